```python
import jax, jax.numpy as jnp
from jax import lax
import numpy as np

D_MODEL = 1024
BATCH = 8
SEQ = 2048
DEPTH = 4
DEC_BATCH = 8
DEC_SEQ = 4096
PAST_LEN = 128

HEAD_DIM = 64
HEADS_PER_GROUP = 4
DILATED_GROUPS = ((128, 1), (512, 4), (2048, 16))
N_GROUPS = 3
ATTN_HEADS = N_GROUPS * HEADS_PER_GROUP
ATTN_WIDTH = ATTN_HEADS * HEAD_DIM
ROPE_DIM = HEAD_DIM // 4
ROPE_THETA = 500000.0
CONV_WIDTH = 3
N_EXPERTS = 16
EXPERT_FF = 2 * D_MODEL
EC_CAPACITY = 2
RMS_EPS = 1e-6
N_ATTN_LAYERS = (DEPTH + 1) // 2
N_CONV_LAYERS = DEPTH // 2
NEG_INF = -1e30

kernel_name = "dilated_attn_shortconv_ec_moe_encoder"


def rms_norm(x, g):
    x32 = x.astype(jnp.float32)
    y = x32 * lax.rsqrt(jnp.mean(x32 * x32, axis=-1, keepdims=True) + RMS_EPS)
    return (y * g.astype(jnp.float32)).astype(x.dtype)


def partial_rope(x):
    S = x.shape[1]
    half = ROPE_DIM // 2
    inv_freq = jnp.power(ROPE_THETA, -jnp.arange(0, ROPE_DIM, 2, dtype=jnp.float32) / ROPE_DIM)
    ang = jnp.arange(S, dtype=jnp.float32)[:, None] * inv_freq[None, :]
    cos = jnp.cos(ang)[None, :, None, :].astype(x.dtype)
    sin = jnp.sin(ang)[None, :, None, :].astype(x.dtype)
    x1 = x[..., :half]
    x2 = x[..., half:ROPE_DIM]
    return jnp.concatenate([x1 * cos - x2 * sin, x2 * cos + x1 * sin, x[..., ROPE_DIM:]], axis=-1)


def dilated_window_attention(q, k, v, window, dilation):
    B, S, H, Dh = q.shape
    radius = window // (2 * dilation)
    L = S // dilation
    qb = radius
    n_blk = -(-L // qb)
    Lq = n_blk * qb
    kw = qb + 2 * radius

    def to_classes(t):
        return t.reshape(B, L, dilation, H, Dh).transpose(0, 2, 3, 1, 4)

    qc = jnp.pad(to_classes(q), ((0, 0), (0, 0), (0, 0), (0, Lq - L), (0, 0)))
    pad_k = ((0, 0), (0, 0), (0, 0), (radius, Lq - L + radius), (0, 0))
    kc = jnp.pad(to_classes(k), pad_k)
    vc = jnp.pad(to_classes(v), pad_k)
    win = jnp.arange(n_blk)[:, None] * qb + jnp.arange(kw)[None, :]
    kb = jnp.take(kc, win, axis=3)
    vb = jnp.take(vc, win, axis=3)
    qbk = qc.reshape(B, dilation, H, n_blk, qb, Dh)
    s = jnp.einsum('brhnqd,brhnkd->brhnqk', qbk, kb, preferred_element_type=jnp.float32)
    rel = jnp.arange(kw)[None, :] - jnp.arange(qb)[:, None]
    key_t = win[:, None, :] - radius
    valid = (rel >= 0) & (rel <= 2 * radius) & (key_t >= 0) & (key_t < L)
    s = jnp.where(valid, s, NEG_INF)
    m = jnp.max(s, axis=-1, keepdims=True)
    p = jnp.exp(s - m)
    l = jnp.sum(p, axis=-1, keepdims=True)
    o = jnp.einsum('brhnqk,brhnkd->brhnqd', p.astype(v.dtype), vb,
                   preferred_element_type=jnp.float32) / l
    lse = (m + jnp.log(l))[..., 0]
    o = o.reshape(B, dilation, H, Lq, Dh)[:, :, :, :L]
    lse = lse.reshape(B, dilation, H, Lq)[:, :, :, :L]
    o = o.transpose(0, 3, 1, 2, 4).reshape(B, S, H, Dh)
    lse = lse.transpose(0, 3, 1, 2).reshape(B, S, H)
    return o, lse


def dilated_attention_mixer(h, w_qkv, w_o):
    B, S, _ = h.shape
    qkv = (h @ w_qkv).reshape(B, S, 3, ATTN_HEADS, HEAD_DIM)
    q = partial_rope(qkv[:, :, 0]) * (HEAD_DIM ** -0.5)
    k = partial_rope(qkv[:, :, 1])
    v = qkv[:, :, 2]
    outs, lses = [], []
    for g, (window, dil) in enumerate(DILATED_GROUPS):
        sl = slice(g * HEADS_PER_GROUP, (g + 1) * HEADS_PER_GROUP)
        o, lse = dilated_window_attention(q[:, :, sl], k[:, :, sl], v[:, :, sl], window, dil)
        outs.append(o)
        lses.append(lse)
    alpha = jax.nn.softmax(jnp.stack(lses, axis=0), axis=0)
    o = jnp.stack(outs, axis=0) * alpha[..., None]
    o = o.transpose(1, 2, 0, 3, 4).reshape(B, S, ATTN_WIDTH).astype(h.dtype)
    return o @ w_o


def short_conv_mixer(h, w_in, conv_w, w_out):
    gb, gc, xt = jnp.split(h @ w_in, 3, axis=-1)
    u = gc * xt
    up = jnp.pad(u, ((0, 0), (1, 1), (0, 0)))
    cw = conv_w.astype(h.dtype)
    conv = cw[0] * up[:, :-2] + cw[1] * up[:, 1:-1] + cw[2] * up[:, 2:]
    return (gb * conv) @ w_out


def expert_choice_ffn(h, w_router, w_gate, w_up, w_down):
    B, S, D = h.shape
    T = B * S
    hf = h.reshape(T, D)
    aff = jax.nn.softmax((hf @ w_router).astype(jnp.float32), axis=-1)
    cap = max(1, (EC_CAPACITY * T) // N_EXPERTS)
    gate, idx = lax.top_k(aff.T, cap)
    xe = hf[idx]
    a = jnp.einsum('ecd,edf->ecf', xe, w_gate)
    b = jnp.einsum('ecd,edf->ecf', xe, w_up)
    ye = jnp.einsum('ecf,efd->ecd', jax.nn.silu(a) * b, w_down)
    ye = ye * gate[..., None].astype(h.dtype)
    out = jnp.zeros((T, D), h.dtype).at[idx.reshape(-1)].add(ye.reshape(-1, D))
    return out.reshape(B, S, D)


def run_trunk(x, attn_norm, w_qkv, w_attn_out, conv_norm, w_conv_in, conv_w, w_conv_out,
              ffn_norm, w_router, w_gate, w_up, w_down, final_norm):
    for i in range(DEPTH):
        j = i // 2
        if i % 2 == 0:
            x = x + dilated_attention_mixer(rms_norm(x, attn_norm[j]), w_qkv[j], w_attn_out[j])
        else:
            x = x + short_conv_mixer(rms_norm(x, conv_norm[j]), w_conv_in[j], conv_w[j], w_conv_out[j])
        x = x + expert_choice_ffn(rms_norm(x, ffn_norm[i]), w_router[i], w_gate[i], w_up[i], w_down[i])
    return rms_norm(x, final_norm)


def setup_inputs(seed: int = 0) -> dict:
    key = jax.random.key(seed)
    ks = jax.random.split(key, 16)
    f32 = jnp.float32
    nrm = lambda k, shape, scale: jax.random.normal(k, shape, f32) * scale
    gain = lambda k, shape: 1.0 + 0.02 * jax.random.normal(k, shape, f32)
    return {
        "x_prompt": jax.random.normal(ks[0], (BATCH, SEQ, D_MODEL), f32),
        "x_sample": jax.random.normal(ks[1], (DEC_BATCH, DEC_SEQ, D_MODEL), f32),
        "attn_norm": gain(ks[2], (N_ATTN_LAYERS, D_MODEL)),
        "w_qkv": nrm(ks[3], (N_ATTN_LAYERS, D_MODEL, 3 * ATTN_WIDTH), D_MODEL ** -0.5),
        "w_attn_out": nrm(ks[4], (N_ATTN_LAYERS, ATTN_WIDTH, D_MODEL), ATTN_WIDTH ** -0.5),
        "conv_norm": gain(ks[5], (N_CONV_LAYERS, D_MODEL)),
        "w_conv_in": nrm(ks[6], (N_CONV_LAYERS, D_MODEL, 3 * D_MODEL), D_MODEL ** -0.5),
        "conv_w": nrm(ks[7], (N_CONV_LAYERS, CONV_WIDTH, D_MODEL), CONV_WIDTH ** -0.5),
        "w_conv_out": nrm(ks[8], (N_CONV_LAYERS, D_MODEL, D_MODEL), D_MODEL ** -0.5),
        "ffn_norm": gain(ks[9], (DEPTH, D_MODEL)),
        "w_router": nrm(ks[10], (DEPTH, D_MODEL, N_EXPERTS), D_MODEL ** -0.5),
        "w_gate": nrm(ks[11], (DEPTH, N_EXPERTS, D_MODEL, EXPERT_FF), D_MODEL ** -0.5),
        "w_up": nrm(ks[12], (DEPTH, N_EXPERTS, D_MODEL, EXPERT_FF), D_MODEL ** -0.5),
        "w_down": nrm(ks[13], (DEPTH, N_EXPERTS, EXPERT_FF, D_MODEL), EXPERT_FF ** -0.5),
        "final_norm": gain(ks[14], (D_MODEL,)),
    }


def reference(x_prompt, x_sample, attn_norm, w_qkv, w_attn_out, conv_norm, w_conv_in, conv_w,
              w_conv_out, ffn_norm, w_router, w_gate, w_up, w_down, final_norm):
    y_prompt = run_trunk(x_prompt, attn_norm, w_qkv, w_attn_out, conv_norm, w_conv_in, conv_w,
                         w_conv_out, ffn_norm, w_router, w_gate, w_up, w_down, final_norm)
    y_sample = run_trunk(x_sample, attn_norm, w_qkv, w_attn_out, conv_norm, w_conv_in, conv_w,
                         w_conv_out, ffn_norm, w_router, w_gate, w_up, w_down, final_norm)
    return (y_prompt, y_sample)
```

```python
import functools

import jax
import jax.numpy as jnp
from jax import lax
from jax.experimental import pallas as pl
from jax.experimental.pallas import tpu as pltpu

D_MODEL = 1024
HEAD_DIM = 64
HEADS_PER_GROUP = 4
DILATED_GROUPS = ((128, 1), (512, 4), (2048, 16))
N_GROUPS = len(DILATED_GROUPS)
ATTN_WIDTH = N_GROUPS * HEADS_PER_GROUP * HEAD_DIM
GROUP_WIDTH = HEADS_PER_GROUP * HEAD_DIM
ROPE_DIM = HEAD_DIM // 4
ROPE_THETA = 500000.0
N_EXPERTS = 16
EC_CAPACITY = 2
RMS_EPS = 1e-6
NEG_INF = -1e30

LANES = 128
ROW_TILE = 512
ATTN_TQ = 128
VMEM_LIMIT = 56 * 1024 * 1024

F32 = jnp.float32
BF16 = jnp.bfloat16


def _params(n_axes):
    return pltpu.CompilerParams(
        dimension_semantics=("arbitrary",) * n_axes, vmem_limit_bytes=VMEM_LIMIT)


def _rms(x, g):
    ms = jnp.mean(x * x, axis=-1, keepdims=True)
    return x * lax.rsqrt(ms + RMS_EPS) * g


def _qkv_kernel(x_ref, g_ref, w_ref, cos_ref, sa_ref, sb_ref, o_ref):
    h = _rms(x_ref[...], g_ref[...]).astype(BF16)
    acc = jnp.dot(h, w_ref[...], preferred_element_type=F32)
    c, sa, sb = cos_ref[...], sa_ref[...], sb_ref[...]
    n_q = ATTN_WIDTH // LANES
    for j in range(2 * n_q):
        blk = acc[:, j * LANES:(j + 1) * LANES]
        r = blk * c + pltpu.roll(blk, LANES - ROPE_DIM // 2, 1) * sa + pltpu.roll(blk, ROPE_DIM // 2, 1) * sb
        if j < n_q:
            r = r * (HEAD_DIM ** -0.5)
        o_ref[:, j * LANES:(j + 1) * LANES] = r.astype(BF16)
    o_ref[:, 2 * ATTN_WIDTH:] = acc[:, 2 * ATTN_WIDTH:].astype(BF16)


def _rope_tables(S):
    half = ROPE_DIM // 2
    inv_freq = jnp.power(ROPE_THETA, -jnp.arange(0, ROPE_DIM, 2, dtype=F32) / ROPE_DIM)
    ang = jnp.arange(S, dtype=F32)[:, None] * inv_freq[None, :]
    cos, sin = jnp.cos(ang), jnp.sin(ang)
    rest = HEAD_DIM - ROPE_DIM
    c = jnp.concatenate([cos, cos, jnp.ones((S, rest), F32)], axis=1)
    sa = jnp.concatenate([-sin, jnp.zeros((S, half + rest), F32)], axis=1)
    sb = jnp.concatenate([jnp.zeros((S, half), F32), sin, jnp.zeros((S, rest), F32)], axis=1)
    rep = LANES // HEAD_DIM
    return tuple(jnp.tile(t, (1, rep)) for t in (c, sa, sb))


def _qkv_proj(x, g, w, tables, S):
    T, D = x.shape
    N = w.shape[1]
    tm = ROW_TILE
    per_seq = S // tm
    tab_spec = pl.BlockSpec((tm, LANES), lambda i: (i % per_seq, 0))
    return pl.pallas_call(
        _qkv_kernel,
        grid=(T // tm,),
        in_specs=[
            pl.BlockSpec((tm, D), lambda i: (i, 0)),
            pl.BlockSpec((1, D), lambda i: (0, 0)),
            pl.BlockSpec((D, N), lambda i: (0, 0)),
            tab_spec, tab_spec, tab_spec,
        ],
        out_specs=pl.BlockSpec((tm, N), lambda i: (i, 0)),
        out_shape=jax.ShapeDtypeStruct((T, N), BF16),
        compiler_params=_params(1),
        name="qkv_proj",
    )(x, g, w, *tables)


def _attn_kernel(q_ref, k_ref, v_ref, o_ref, l_ref, *, L, Lq, kw, tq, radius):
    c = pl.program_id(2)

    def body(i, carry):
        r0 = pl.multiple_of(i * tq, tq)
        t0 = c * Lq + i * tq
        ks = pl.multiple_of(jnp.clip(t0 - radius, 0, L - kw), radius)
        qb = q_ref[pl.ds(r0, tq), :]
        kb = k_ref[pl.ds(ks, kw), :]
        vb = v_ref[pl.ds(ks, kw), :]
        qi = t0 + lax.broadcasted_iota(jnp.int32, (tq, kw), 0)
        kj = ks + lax.broadcasted_iota(jnp.int32, (tq, kw), 1)
        valid = jnp.abs(qi - kj) <= radius
        outs, lses = [], []
        for h in range(HEADS_PER_GROUP):
            sl = slice(h * HEAD_DIM, (h + 1) * HEAD_DIM)
            s = lax.dot_general(qb[:, sl], kb[:, sl], (((1,), (1,)), ((), ())),
                                preferred_element_type=F32)
            s = jnp.where(valid, s, NEG_INF)
            m = jnp.max(s, axis=-1, keepdims=True)
            p = jnp.exp(s - m)
            l = jnp.sum(p, axis=-1, keepdims=True)
            o = jnp.dot(p.astype(BF16), vb[:, sl], preferred_element_type=F32) / l
            outs.append(o)
            lses.append(jnp.broadcast_to(m + jnp.log(l), (tq, HEAD_DIM)))
        o_ref[pl.ds(r0, tq), :] = jnp.concatenate(outs, axis=1)
        l_ref[pl.ds(r0, tq), :] = jnp.concatenate(lses, axis=1)
        return carry

    lax.fori_loop(0, Lq // tq, body, 0)


def _dilated_attention(qkv, B, S, group):
    window, dil = DILATED_GROUPS[group]
    radius = window // (2 * dil)
    L = S // dil
    tq = min(ATTN_TQ, L)
    kw = min(L, tq + 2 * radius)
    Lq = min(L, 1024)
    assert L % tq == 0 and L % Lq == 0 and Lq % tq == 0 and radius % 16 == 0 and (L - kw) % radius == 0
    nq = 3 * ATTN_WIDTH // GROUP_WIDTH
    x = qkv.reshape(B, L, dil * 3 * ATTN_WIDTH)
    n_g = ATTN_WIDTH // GROUP_WIDTH
    kern = functools.partial(_attn_kernel, L=L, Lq=Lq, kw=kw, tq=tq, radius=radius)
    out_spec = pl.BlockSpec((None, Lq, GROUP_WIDTH), lambda b, r, c: (b, c, r))
    o, lse = pl.pallas_call(
        kern,
        grid=(B, dil, L // Lq),
        in_specs=[
            pl.BlockSpec((None, Lq, GROUP_WIDTH), lambda b, r, c: (b, c, r * nq + group)),
            pl.BlockSpec((None, L, GROUP_WIDTH), lambda b, r, c: (b, 0, r * nq + n_g + group)),
            pl.BlockSpec((None, L, GROUP_WIDTH), lambda b, r, c: (b, 0, r * nq + 2 * n_g + group)),
        ],
        out_specs=[out_spec, out_spec],
        out_shape=[jax.ShapeDtypeStruct((B, L, dil * GROUP_WIDTH), F32)] * 2,
        compiler_params=_params(3),
        name=f"dilated_attn_g{group}",
    )(x, x, x)
    return o.reshape(B * S, GROUP_WIDTH), lse.reshape(B * S, GROUP_WIDTH)


def _oproj_kernel(o0_ref, o1_ref, o2_ref, l0_ref, l1_ref, l2_ref, w_ref, x_ref, y_ref):
    l0, l1, l2 = l0_ref[...], l1_ref[...], l2_ref[...]
    m = jnp.maximum(jnp.maximum(l0, l1), l2)
    e0, e1, e2 = jnp.exp(l0 - m), jnp.exp(l1 - m), jnp.exp(l2 - m)
    den = e0 + e1 + e2
    z = jnp.concatenate(
        [o0_ref[...] * (e0 / den), o1_ref[...] * (e1 / den), o2_ref[...] * (e2 / den)], axis=1)
    y_ref[...] = x_ref[...] + jnp.dot(z.astype(BF16), w_ref[...], preferred_element_type=F32)


def _attn_out_proj(outs, lses, w, x):
    T, D = x.shape
    tm = ROW_TILE
    gspec = pl.BlockSpec((tm, GROUP_WIDTH), lambda i: (i, 0))
    xspec = pl.BlockSpec((tm, D), lambda i: (i, 0))
    return pl.pallas_call(
        _oproj_kernel,
        grid=(T // tm,),
        in_specs=[gspec] * 6 + [pl.BlockSpec((ATTN_WIDTH, D), lambda i: (0, 0)), xspec],
        out_specs=xspec,
        out_shape=jax.ShapeDtypeStruct((T, D), F32),
        compiler_params=_params(1),
        name="attn_out_proj",
    )(*outs, *lses, w, x)


def _conv_kernel(x_ref, xp_ref, xn_ref, g_ref, win_ref, cw_ref, wout_ref, y_ref, u_scr, *, tm, per_seq):
    i = pl.program_id(0)
    D = x_ref.shape[1]
    x = x_ref[...]
    xa = jnp.concatenate([xp_ref[...], x, xn_ref[...]], axis=0)
    h = _rms(xa, g_ref[...]).astype(BF16)
    full = jnp.dot(h, win_ref[...], preferred_element_type=F32)
    u_scr[...] = full[:, D:2 * D] * full[:, 2 * D:]
    first = (i % per_seq) == 0
    last = (i % per_seq) == per_seq - 1
    u_scr[pl.ds(7, 1), :] = jnp.where(first, 0.0, u_scr[pl.ds(7, 1), :])
    u_scr[pl.ds(tm + 8, 1), :] = jnp.where(last, 0.0, u_scr[pl.ds(tm + 8, 1), :])
    cw = cw_ref[...]
    conv = (cw[0:1] * u_scr[pl.ds(7, tm), :] + cw[1:2] * u_scr[pl.ds(8, tm), :]
            + cw[2:3] * u_scr[pl.ds(9, tm), :])
    z = (full[8:8 + tm, :D] * conv).astype(BF16)
    y_ref[...] = x + jnp.dot(z, wout_ref[...], preferred_element_type=F32)


def _short_conv(x, g, w_in, cw, w_out, S):
    T, D = x.shape
    tm = ROW_TILE
    per_seq = S // tm
    sub = tm // 8
    n8 = T // 8
    kern = functools.partial(_conv_kernel, tm=tm, per_seq=per_seq)
    return pl.pallas_call(
        kern,
        grid=(T // tm,),
        in_specs=[
            pl.BlockSpec((tm, D), lambda i: (i, 0)),
            pl.BlockSpec((8, D), lambda i: (jnp.maximum(i * sub - 1, 0), 0)),
            pl.BlockSpec((8, D), lambda i: (jnp.minimum((i + 1) * sub, n8 - 1), 0)),
            pl.BlockSpec((1, D), lambda i: (0, 0)),
            pl.BlockSpec((D, 3 * D), lambda i: (0, 0)),
            pl.BlockSpec((3, D), lambda i: (0, 0)),
            pl.BlockSpec((D, D), lambda i: (0, 0)),
        ],
        out_specs=pl.BlockSpec((tm, D), lambda i: (i, 0)),
        out_shape=jax.ShapeDtypeStruct((T, D), F32),
        scratch_shapes=[pltpu.VMEM((tm + 16, D), F32)],
        compiler_params=_params(1),
        name="short_conv",
    )(x, x, x, g, w_in, cw, w_out)


def _router_kernel(x_ref, g_ref, wr_ref, hn_ref, aff_ref):
    h = _rms(x_ref[...], g_ref[...])
    hn_ref[...] = h
    logits = lax.dot_general(wr_ref[...], h.astype(BF16), (((1,), (1,)), ((), ())),
                             preferred_element_type=F32)
    m = jnp.max(logits, axis=0, keepdims=True)
    e = jnp.exp(logits - m)
    aff_ref[...] = e / jnp.sum(e, axis=0, keepdims=True)


def _router(x, g, w_rt):
    T, D = x.shape
    E = w_rt.shape[0]
    tm = ROW_TILE
    return pl.pallas_call(
        _router_kernel,
        grid=(T // tm,),
        in_specs=[
            pl.BlockSpec((tm, D), lambda i: (i, 0)),
            pl.BlockSpec((1, D), lambda i: (0, 0)),
            pl.BlockSpec((E, D), lambda i: (0, 0)),
        ],
        out_specs=[pl.BlockSpec((tm, D), lambda i: (i, 0)), pl.BlockSpec((E, tm), lambda i: (0, i))],
        out_shape=[jax.ShapeDtypeStruct((T, D), F32), jax.ShapeDtypeStruct((E, T), F32)],
        compiler_params=_params(1),
        name="router",
    )(x, g, w_rt)


def _ffn_kernel(x_ref, gate_ref, wg_ref, wu_ref, wd_ref, y_ref):
    tm = x_ref.shape[0]
    x = x_ref[...].astype(BF16)
    a = jnp.dot(x, wg_ref[...], preferred_element_type=F32)
    b = jnp.dot(x, wu_ref[...], preferred_element_type=F32)
    h = (a * (1.0 / (1.0 + jnp.exp(-a)))) * b
    y = jnp.dot(h.astype(BF16), wd_ref[...], preferred_element_type=F32)
    ri = lax.broadcasted_iota(jnp.int32, (tm, tm), 0)
    ci = lax.broadcasted_iota(jnp.int32, (tm, tm), 1)
    gcol = jnp.sum(jnp.where(ri == ci, gate_ref[...], 0.0), axis=1, keepdims=True)
    y_ref[...] = y * gcol


def _expert_ffn(xe, gate, wg, wu, wd):
    E, C, D = xe.shape
    F = wg.shape[2]
    tm = min(ROW_TILE, C)
    return pl.pallas_call(
        _ffn_kernel,
        grid=(E, C // tm),
        in_specs=[
            pl.BlockSpec((None, tm, D), lambda e, i: (e, i, 0)),
            pl.BlockSpec((None, 1, tm), lambda e, i: (e, 0, i)),
            pl.BlockSpec((None, D, F), lambda e, i: (e, 0, 0)),
            pl.BlockSpec((None, D, F), lambda e, i: (e, 0, 0)),
            pl.BlockSpec((None, F, D), lambda e, i: (e, 0, 0)),
        ],
        out_specs=pl.BlockSpec((None, tm, D), lambda e, i: (e, i, 0)),
        out_shape=jax.ShapeDtypeStruct((E, C, D), F32),
        compiler_params=_params(2),
        name="expert_ffn",
    )(xe, gate.reshape(E, 1, C), wg, wu, wd)


def _norm_kernel(x_ref, g_ref, y_ref):
    y_ref[...] = _rms(x_ref[...], g_ref[...])


def _final_norm(x, g):
    T, D = x.shape
    tm = ROW_TILE
    return pl.pallas_call(
        _norm_kernel,
        grid=(T // tm,),
        in_specs=[pl.BlockSpec((tm, D), lambda i: (i, 0)), pl.BlockSpec((1, D), lambda i: (0, 0))],
        out_specs=pl.BlockSpec((tm, D), lambda i: (i, 0)),
        out_shape=jax.ShapeDtypeStruct((T, D), F32),
        compiler_params=_params(1),
        name="final_norm",
    )(x, g)


def _moe(x, g, w_rt, wg, wu, wd):
    T, D = x.shape
    hn, aff_t = _router(x, g, w_rt)
    cap = max(1, (EC_CAPACITY * T) // N_EXPERTS)
    gate, idx = lax.top_k(aff_t, cap)
    xe = hn[idx]
    ye = _expert_ffn(xe, gate, wg, wu, wd)
    return x.at[idx.reshape(-1)].add(ye.reshape(-1, D))


def _trunk(x3, p):
    B, S, D = x3.shape
    x = x3.reshape(B * S, D)
    tables = _rope_tables(S)
    depth = p["ffn_norm"].shape[0]
    for i in range(depth):
        j = i // 2
        if i % 2 == 0:
            qkv = _qkv_proj(x, p["attn_norm"][j][None], p["w_qkv"][j], tables, S)
            res = [_dilated_attention(qkv, B, S, g) for g in range(N_GROUPS)]
            x = _attn_out_proj([r[0] for r in res], [r[1] for r in res], p["w_attn_out"][j], x)
        else:
            x = _short_conv(x, p["conv_norm"][j][None], p["w_conv_in"][j], p["conv_w"][j],
                            p["w_conv_out"][j], S)
        x = _moe(x, p["ffn_norm"][i][None], p["w_router_t"][i], p["w_gate"][i], p["w_up"][i], p["w_down"][i])
    return _final_norm(x, p["final_norm"][None]).reshape(B, S, D)


def kernel(x_prompt, x_sample, attn_norm, w_qkv, w_attn_out, conv_norm, w_conv_in, conv_w, w_conv_out,
           ffn_norm, w_router, w_gate, w_up, w_down, final_norm):
    p = dict(
        attn_norm=attn_norm, conv_norm=conv_norm, ffn_norm=ffn_norm, final_norm=final_norm, conv_w=conv_w,
        w_qkv=w_qkv.astype(BF16), w_attn_out=w_attn_out.astype(BF16),
        w_conv_in=w_conv_in.astype(BF16), w_conv_out=w_conv_out.astype(BF16),
        w_router_t=jnp.swapaxes(w_router, 1, 2).astype(BF16),
        w_gate=w_gate.astype(BF16), w_up=w_up.astype(BF16), w_down=w_down.astype(BF16),
    )
    return (_trunk(x_prompt, p), _trunk(x_sample, p))
```

```python
import functools

import jax
import jax.numpy as jnp
from jax import lax
from jax.experimental import pallas as pl
from jax.experimental.pallas import tpu as pltpu

D_MODEL = 1024
HEAD_DIM = 64
HEADS_PER_GROUP = 4
DILATED_GROUPS = ((128, 1), (512, 4), (2048, 16))
N_GROUPS = len(DILATED_GROUPS)
ATTN_WIDTH = N_GROUPS * HEADS_PER_GROUP * HEAD_DIM
GROUP_WIDTH = HEADS_PER_GROUP * HEAD_DIM
ROPE_DIM = HEAD_DIM // 4
ROPE_THETA = 500000.0
N_EXPERTS = 16
EC_CAPACITY = 2
RMS_EPS = 1e-6
NEG_INF = -1e30

LANES = 128
SUB = 8
ROW_TILE = 512
ATTN_TQ = 128
MOE_TILE = 256
CHUNK = 256
PIECE = 16
LOG2_PIECE = 4
LOG2_CHUNK = 8
LOG2_SUB = 3
assert (1 << LOG2_PIECE, 1 << LOG2_CHUNK, 1 << LOG2_SUB) == (PIECE, CHUNK, SUB)
VMEM_LIMIT = 56 * 1024 * 1024

F32 = jnp.float32
BF16 = jnp.bfloat16


def _params(n_axes):
    return pltpu.CompilerParams(
        dimension_semantics=("arbitrary",) * n_axes, vmem_limit_bytes=VMEM_LIMIT)


def _rms(x, g):
    ms = jnp.mean(x * x, axis=-1, keepdims=True)
    return x * lax.rsqrt(ms + RMS_EPS) * g


def _qkv_kernel(x_ref, g_ref, w_ref, cos_ref, sa_ref, sb_ref, o0_ref, o1_ref, o2_ref, s_ref):
    tm = x_ref.shape[0]
    h = _rms(x_ref[...], g_ref[...]).astype(BF16)
    acc = jnp.dot(h, w_ref[...], preferred_element_type=F32)
    c, sa, sb = cos_ref[...], sa_ref[...], sb_ref[...]
    n_q = ATTN_WIDTH // LANES
    for j in range(2 * n_q):
        blk = acc[:, j * LANES:(j + 1) * LANES]
        r = blk * c + pltpu.roll(blk, LANES - ROPE_DIM // 2, 1) * sa + pltpu.roll(blk, ROPE_DIM // 2, 1) * sb
        if j < n_q:
            r = r * (HEAD_DIM ** -0.5)
        s_ref[j] = r
    for j in range(2 * n_q, 3 * n_q):
        s_ref[j] = acc[:, j * LANES:(j + 1) * LANES]
    per_group = GROUP_WIDTH // LANES
    for g, o_ref in enumerate((o0_ref, o1_ref, o2_ref)):
        dil = DILATED_GROUPS[g][1]
        for part in range(3):
            for jj in range(per_group):
                src = part * n_q + g * per_group + jj
                dst = slice((part * per_group + jj) * LANES, (part * per_group + jj + 1) * LANES)
                for r in range(dil):
                    rows = pl.ds(r, tm // dil, stride=dil) if dil > 1 else slice(None)
                    o_ref[r, :, dst] = s_ref[src, rows, :].astype(BF16)


def _rope_tables(S):
    half = ROPE_DIM // 2
    inv_freq = jnp.power(ROPE_THETA, -jnp.arange(0, ROPE_DIM, 2, dtype=F32) / ROPE_DIM)
    ang = jnp.arange(S, dtype=F32)[:, None] * inv_freq[None, :]
    cos, sin = jnp.cos(ang), jnp.sin(ang)
    rest = HEAD_DIM - ROPE_DIM
    c = jnp.concatenate([cos, cos, jnp.ones((S, rest), F32)], axis=1)
    sa = jnp.concatenate([-sin, jnp.zeros((S, half + rest), F32)], axis=1)
    sb = jnp.concatenate([jnp.zeros((S, half), F32), sin, jnp.zeros((S, rest), F32)], axis=1)
    rep = LANES // HEAD_DIM
    return tuple(jnp.tile(t, (1, rep)) for t in (c, sa, sb))


def _qkv_proj(x, g, w, tables, B, S):
    T, D = x.shape
    N = w.shape[1]
    tm = ROW_TILE
    per_seq = S // tm
    tab_spec = pl.BlockSpec((tm, LANES), lambda i: (i % per_seq, 0))
    dils = [d for _, d in DILATED_GROUPS]
    return pl.pallas_call(
        _qkv_kernel,
        grid=(T // tm,),
        in_specs=[
            pl.BlockSpec((tm, D), lambda i: (i, 0)),
            pl.BlockSpec((1, D), lambda i: (0, 0)),
            pl.BlockSpec((D, N), lambda i: (0, 0)),
            tab_spec, tab_spec, tab_spec,
        ],
        out_specs=[pl.BlockSpec((None, d, tm // d, 3 * GROUP_WIDTH), lambda i: (i // per_seq, 0, i % per_seq, 0))
                   for d in dils],
        out_shape=[jax.ShapeDtypeStruct((B, d, S // d, 3 * GROUP_WIDTH), BF16) for d in dils],
        scratch_shapes=[pltpu.VMEM((N // LANES, tm, LANES), F32)],
        compiler_params=_params(1),
        name="qkv_proj",
    )(x, g, w, *tables)


def _attn_kernel(q_ref, k_ref, v_ref, o_ref, l_ref, *, L, Lq, kw, tq, radius):
    c = pl.program_id(2)

    def body(i, carry):
        r0 = pl.multiple_of(i * tq, tq)
        t0 = c * Lq + i * tq
        ks = pl.multiple_of(jnp.clip(t0 - radius, 0, L - kw), radius)
        qb = q_ref[pl.ds(r0, tq), :]
        kb = k_ref[pl.ds(ks, kw), :]
        vb = v_ref[pl.ds(ks, kw), :]
        qi = t0 + lax.broadcasted_iota(jnp.int32, (tq, kw), 0)
        kj = ks + lax.broadcasted_iota(jnp.int32, (tq, kw), 1)
        valid = jnp.abs(qi - kj) <= radius
        outs, lses = [], []
        for h in range(HEADS_PER_GROUP):
            sl = slice(h * HEAD_DIM, (h + 1) * HEAD_DIM)
            s = lax.dot_general(qb[:, sl], kb[:, sl], (((1,), (1,)), ((), ())),
                                preferred_element_type=F32)
            s = jnp.where(valid, s, NEG_INF)
            m = jnp.max(s, axis=-1, keepdims=True)
            p = jnp.exp(s - m)
            l = jnp.sum(p, axis=-1, keepdims=True)
            o = jnp.dot(p.astype(BF16), vb[:, sl], preferred_element_type=F32) / l
            outs.append(o)
            lses.append(jnp.broadcast_to(m + jnp.log(l), (tq, HEAD_DIM)))
        o_ref[pl.ds(r0, tq), :] = jnp.concatenate(outs, axis=1)
        l_ref[pl.ds(r0, tq), :] = jnp.concatenate(lses, axis=1)
        return carry

    lax.fori_loop(0, Lq // tq, body, 0)


def _dilated_attention(qkv, group):
    window, dil = DILATED_GROUPS[group]
    B, _, L, _ = qkv.shape
    radius = window // (2 * dil)
    tq = min(ATTN_TQ, L)
    kw = min(L, tq + 2 * radius)
    Lq = min(L, 1024)
    assert L % tq == 0 and L % Lq == 0 and Lq % tq == 0 and radius % 16 == 0 and (L - kw) % radius == 0
    kern = functools.partial(_attn_kernel, L=L, Lq=Lq, kw=kw, tq=tq, radius=radius)
    out_spec = pl.BlockSpec((None, None, Lq, GROUP_WIDTH), lambda b, r, c: (b, r, c, 0))
    return pl.pallas_call(
        kern,
        grid=(B, dil, L // Lq),
        in_specs=[
            pl.BlockSpec((None, None, Lq, GROUP_WIDTH), lambda b, r, c: (b, r, c, 0)),
            pl.BlockSpec((None, None, L, GROUP_WIDTH), lambda b, r, c: (b, r, 0, 1)),
            pl.BlockSpec((None, None, L, GROUP_WIDTH), lambda b, r, c: (b, r, 0, 2)),
        ],
        out_specs=[out_spec, out_spec],
        out_shape=[jax.ShapeDtypeStruct((B, dil, L, GROUP_WIDTH), F32)] * 2,
        compiler_params=_params(3),
        name=f"dilated_attn_g{group}",
    )(qkv, qkv, qkv)


def _oproj_kernel(o0_ref, o1_ref, o2_ref, l0_ref, l1_ref, l2_ref, w_ref, x_ref, y_ref, o_scr, l_scr):
    tm = x_ref.shape[0]
    for g, (o_ref, l_ref) in enumerate(((o0_ref, l0_ref), (o1_ref, l1_ref), (o2_ref, l2_ref))):
        dil = DILATED_GROUPS[g][1]
        for r in range(dil):
            rows = pl.ds(r, tm // dil, stride=dil) if dil > 1 else slice(None)
            for jj in range(GROUP_WIDTH // LANES):
                lanes = slice(jj * LANES, (jj + 1) * LANES)
                o_scr[g, jj, rows, :] = o_ref[r, :, lanes]
                l_scr[g, jj, rows, :] = l_ref[r, :, lanes]
    zs = []
    for jj in range(GROUP_WIDTH // LANES):
        l0, l1, l2 = l_scr[0, jj], l_scr[1, jj], l_scr[2, jj]
        m = jnp.maximum(jnp.maximum(l0, l1), l2)
        e0, e1, e2 = jnp.exp(l0 - m), jnp.exp(l1 - m), jnp.exp(l2 - m)
        den = e0 + e1 + e2
        zs.append((o_scr[0, jj] * (e0 / den), o_scr[1, jj] * (e1 / den), o_scr[2, jj] * (e2 / den)))
    z = jnp.concatenate([zs[jj][g] for g in range(N_GROUPS) for jj in range(GROUP_WIDTH // LANES)], axis=1)
    y_ref[...] = x_ref[...] + jnp.dot(z.astype(BF16), w_ref[...], preferred_element_type=F32)


def _attn_out_proj(outs, lses, w, x, S):
    T, D = x.shape
    tm = ROW_TILE
    per_seq = S // tm
    gspecs = [pl.BlockSpec((None, d, tm // d, GROUP_WIDTH), lambda i: (i // per_seq, 0, i % per_seq, 0))
              for _, d in DILATED_GROUPS]
    xspec = pl.BlockSpec((tm, D), lambda i: (i, 0))
    return pl.pallas_call(
        _oproj_kernel,
        grid=(T // tm,),
        in_specs=gspecs + gspecs + [pl.BlockSpec((ATTN_WIDTH, D), lambda i: (0, 0)), xspec],
        out_specs=xspec,
        out_shape=jax.ShapeDtypeStruct((T, D), F32),
        scratch_shapes=[pltpu.VMEM((N_GROUPS, GROUP_WIDTH // LANES, tm, LANES), F32)] * 2,
        compiler_params=_params(1),
        name="attn_out_proj",
    )(*outs, *lses, w, x)


def _conv_kernel(x_ref, xp_ref, xn_ref, g_ref, win_ref, cw_ref, wout_ref, y_ref, u_scr, *, tm, per_seq):
    i = pl.program_id(0)
    D = x_ref.shape[1]
    x = x_ref[...]
    xa = jnp.concatenate([xp_ref[...], x, xn_ref[...]], axis=0)
    h = _rms(xa, g_ref[...]).astype(BF16)
    full = jnp.dot(h, win_ref[...], preferred_element_type=F32)
    u_scr[...] = full[:, D:2 * D] * full[:, 2 * D:]
    first = (i % per_seq) == 0
    last = (i % per_seq) == per_seq - 1
    u_scr[pl.ds(7, 1), :] = jnp.where(first, 0.0, u_scr[pl.ds(7, 1), :])
    u_scr[pl.ds(tm + 8, 1), :] = jnp.where(last, 0.0, u_scr[pl.ds(tm + 8, 1), :])
    cw = cw_ref[...]
    conv = (cw[0:1] * u_scr[pl.ds(7, tm), :] + cw[1:2] * u_scr[pl.ds(8, tm), :]
            + cw[2:3] * u_scr[pl.ds(9, tm), :])
    z = (full[8:8 + tm, :D] * conv).astype(BF16)
    y_ref[...] = x + jnp.dot(z, wout_ref[...], preferred_element_type=F32)


def _short_conv(x, g, w_in, cw, w_out, S):
    T, D = x.shape
    tm = ROW_TILE
    per_seq = S // tm
    sub = tm // 8
    n8 = T // 8
    kern = functools.partial(_conv_kernel, tm=tm, per_seq=per_seq)
    return pl.pallas_call(
        kern,
        grid=(T // tm,),
        in_specs=[
            pl.BlockSpec((tm, D), lambda i: (i, 0)),
            pl.BlockSpec((8, D), lambda i: (jnp.maximum(i * sub - 1, 0), 0)),
            pl.BlockSpec((8, D), lambda i: (jnp.minimum((i + 1) * sub, n8 - 1), 0)),
            pl.BlockSpec((1, D), lambda i: (0, 0)),
            pl.BlockSpec((D, 3 * D), lambda i: (0, 0)),
            pl.BlockSpec((3, D), lambda i: (0, 0)),
            pl.BlockSpec((D, D), lambda i: (0, 0)),
        ],
        out_specs=pl.BlockSpec((tm, D), lambda i: (i, 0)),
        out_shape=jax.ShapeDtypeStruct((T, D), F32),
        scratch_shapes=[pltpu.VMEM((tm + 16, D), F32)],
        compiler_params=_params(1),
        name="short_conv",
    )(x, x, x, g, w_in, cw, w_out)


def _router_kernel(x_ref, g_ref, wr_ref, hn_ref, aff_ref):
    h = _rms(x_ref[...], g_ref[...]).astype(BF16)
    hn_ref[...] = h
    logits = lax.dot_general(wr_ref[...], h, (((1,), (1,)), ((), ())),
                             preferred_element_type=F32)
    m = jnp.max(logits, axis=0, keepdims=True)
    e = jnp.exp(logits - m)
    aff = e / jnp.sum(e, axis=0, keepdims=True)
    for k in range(aff_ref.shape[0]):
        aff_ref[k] = aff[:, k * MOE_TILE:(k + 1) * MOE_TILE]


def _router(x, g, w_rt):
    T, D = x.shape
    E = w_rt.shape[0]
    tm = ROW_TILE
    per = tm // MOE_TILE
    return pl.pallas_call(
        _router_kernel,
        grid=(T // tm,),
        in_specs=[
            pl.BlockSpec((tm, D), lambda i: (i, 0)),
            pl.BlockSpec((1, D), lambda i: (0, 0)),
            pl.BlockSpec((E, D), lambda i: (0, 0)),
        ],
        out_specs=[pl.BlockSpec((tm, D), lambda i: (i, 0)),
                   pl.BlockSpec((per, E, MOE_TILE), lambda i: (i, 0, 0))],
        out_shape=[jax.ShapeDtypeStruct((T, D), BF16),
                   jax.ShapeDtypeStruct((T // MOE_TILE, E, MOE_TILE), F32)],
        compiler_params=_params(1),
        name="router",
    )(x, g, w_rt)


def _strict_upper(n):
    r = lax.broadcasted_iota(jnp.int32, (n, n), 0)
    c = lax.broadcasted_iota(jnp.int32, (n, n), 1)
    return (r < c).astype(BF16)


def _select_kernel(aff_ref, sel_ref, cnt_ref, pos_ref, *, cap):
    NT, E, tm = aff_ref.shape
    capf = float(cap)

    def count_ge(v):
        bits = lax.bitcast_convert_type(aff_ref[...], jnp.int32)
        ge = (bits >= v[None]).astype(F32)
        return jnp.sum(jnp.sum(ge, axis=0), axis=1, keepdims=True)

    def bisect(_, lh):
        lo, hi = lh
        mid = lo + lax.shift_right_logical(hi - lo, 1)
        ok = count_ge(mid) >= capf
        return jnp.where(ok, mid, lo), jnp.where(ok, hi, mid)

    one_bits = 0x3F800000
    lo0 = jnp.zeros((E, 1), jnp.int32)
    hi0 = jnp.full((E, 1), one_bits + 1, jnp.int32)
    tau, _ = lax.fori_loop(0, 31, bisect, (lo0, hi0))
    need = capf - count_ge(tau + 1)
    tri = _strict_upper(tm)

    def tile(j, carry):
        ceq, csel = carry
        bits = lax.bitcast_convert_type(aff_ref[j], jnp.int32)
        gt = bits > tau
        eq = (bits == tau).astype(F32)
        rank_eq = jnp.dot(eq.astype(BF16), tri, preferred_element_type=F32) + ceq
        sel = jnp.where(gt | ((eq > 0) & (rank_eq < need)), 1.0, 0.0)
        sel_ref[j] = sel
        c = jnp.sum(sel, axis=1, keepdims=True)
        cnt_ref[j] = jnp.broadcast_to(c, (E, LANES))
        pos_ref[j] = jnp.broadcast_to(csel, (E, LANES))
        return ceq + jnp.sum(eq, axis=1, keepdims=True), csel + c

    zero = jnp.zeros((E, 1), F32)
    lax.fori_loop(0, NT, tile, (zero, zero))


def _select(aff, cap):
    NT, E, tm = aff.shape
    sel, cnt, pos = pl.pallas_call(
        functools.partial(_select_kernel, cap=cap),
        out_shape=[jax.ShapeDtypeStruct((NT, E, tm), F32),
                   jax.ShapeDtypeStruct((NT, E, LANES), F32),
                   jax.ShapeDtypeStruct((NT, E, LANES), F32)],
        compiler_params=pltpu.CompilerParams(vmem_limit_bytes=VMEM_LIMIT),
        name="expert_select",
    )(aff)
    to_smem = lambda a: a[:, :, 0].astype(jnp.int32).reshape(NT * E)
    return sel, to_smem(cnt), to_smem(pos)


def _stack_rows(E, tm):
    worst = E * tm + E * (SUB - 1 + PIECE - 1)
    return -(-worst // CHUNK) * CHUNK


def _cdiv_pow2(x, log2):
    return lax.shift_right_logical(x + ((1 << log2) - 1), log2)


def _segments(cnt_ref, pos_ref, tile, E):
    segs = []
    base = jnp.int32(0)
    for e in range(E):
        n = cnt_ref[tile * E + e]
        p0 = pos_ref[tile * E + e]
        a = p0 & (SUB - 1)
        pieces = jnp.where(n > 0, _cdiv_pow2(a + n, LOG2_PIECE), 0)
        segs.append((base, pieces, p0 - a, a, n))
        base = base + pieces * PIECE
    return segs, base


def _stack_columns(segs, sel, tm):
    E = sel.shape[0]
    rank = jnp.dot(sel.astype(BF16), _strict_upper(tm), preferred_element_type=F32)
    eidx = lax.broadcasted_iota(jnp.int32, (E, 1), 0)
    off = jnp.zeros((E, 1), F32)
    for e in range(E):
        off = jnp.where(eidx == e, (segs[e][0] + segs[e][3]).astype(F32), off)
    return jnp.where(sel > 0, off + rank, -1.0)


def _total_pieces(segs):
    t = segs[0][1]
    for s in segs[1:]:
        t = t + s[1]
    return t


def _wait_pieces(n, piece_copy):
    def w(_, c):
        piece_copy.wait()
        return c
    lax.fori_loop(0, n, w, 0)


def _dispatch_kernel(cnt_ref, pos_ref, sel_ref, hn_ref, xe_ref, stack, pend, sem):
    i = pl.program_id(0)
    nt = pl.num_programs(0)
    slot = i % 2
    E, tm = sel_ref.shape

    @pl.when(i == 0)
    def _():
        pend[...] = jnp.zeros_like(pend)

    segs, total = _segments(cnt_ref, pos_ref, i, E)
    col = _stack_columns(segs, sel_ref[...], tm)

    def chunk(k, c):
        j0 = pl.multiple_of(k * CHUNK, CHUNK)
        jrow = (lax.broadcasted_iota(jnp.int32, (CHUNK, tm), 0) + j0).astype(F32)
        oh = jnp.zeros((CHUNK, tm), F32)
        for e in range(E):
            oh = jnp.where(jrow == col[e:e + 1, :], 1.0, oh)
        stack[slot, pl.ds(j0, CHUNK), :] = jnp.dot(oh.astype(BF16), hn_ref[...], preferred_element_type=F32)
        return c

    lax.fori_loop(0, _cdiv_pow2(total, LOG2_CHUNK), chunk, 0)

    for e in range(E):
        base, _, _, a, n = segs[e]

        @pl.when(n > 0)
        def _():
            first = pl.multiple_of(base, PIECE)
            stack[slot, pl.ds(first, SUB), :] = stack[slot, pl.ds(first, SUB), :] + pend[e]
            last = pl.multiple_of(base + lax.shift_right_logical(a + n - 1, LOG2_SUB) * SUB, SUB)
            incomplete = ((a + n) & (SUB - 1)) != 0
            pend[e] = jnp.where(incomplete, stack[slot, pl.ds(last, SUB), :], 0.0)

    def piece_copy(s, src_row, e, dst_row):
        return pltpu.make_async_copy(stack.at[s, pl.ds(src_row, PIECE), :],
                                     xe_ref.at[e, pl.ds(dst_row, PIECE), :], sem.at[s])

    @pl.when(i > 0)
    def _():
        prev, _ = _segments(cnt_ref, pos_ref, i - 1, E)
        _wait_pieces(_total_pieces(prev), piece_copy(1 - slot, 0, 0, 0))

    for e in range(E):
        base, pieces, start, _, _ = segs[e]

        def issue(c, carry, base=base, start=start, e=e):
            piece_copy(slot, pl.multiple_of(base + c * PIECE, PIECE), e,
                       pl.multiple_of(start + c * PIECE, SUB)).start()
            return carry

        lax.fori_loop(0, pieces, issue, 0)

    @pl.when(i == nt - 1)
    def _():
        _wait_pieces(_total_pieces(segs), piece_copy(slot, 0, 0, 0))
        cap = xe_ref.shape[1] - PIECE
        stack[slot, pl.ds(0, PIECE), :] = jnp.zeros((PIECE, stack.shape[2]), F32)
        for e in range(E):
            piece_copy(slot, 0, e, cap).start()
        _wait_pieces(E, piece_copy(slot, 0, 0, 0))


def _dispatch(sel, cnt, pos, hn, cap):
    NT, E, tm = sel.shape
    T, D = hn.shape
    return pl.pallas_call(
        _dispatch_kernel,
        grid_spec=pltpu.PrefetchScalarGridSpec(
            num_scalar_prefetch=2,
            grid=(NT,),
            in_specs=[pl.BlockSpec((None, E, tm), lambda i, c, p: (i, 0, 0)),
                      pl.BlockSpec((tm, D), lambda i, c, p: (i, 0))],
            out_specs=pl.BlockSpec(memory_space=pl.ANY),
            scratch_shapes=[pltpu.VMEM((2, _stack_rows(E, tm), D), F32),
                            pltpu.VMEM((E, SUB, D), F32),
                            pltpu.SemaphoreType.DMA((2,))]),
        out_shape=jax.ShapeDtypeStruct((E, cap + PIECE, D), F32),
        compiler_params=_params(1),
        name="expert_dispatch",
    )(cnt, pos, sel, hn)


def _combine_kernel(cnt_ref, pos_ref, sel_ref, aff_ref, x_ref, y_hbm, o_ref, stack, sem):
    i = pl.program_id(0)
    nt = pl.num_programs(0)
    slot = i % 2
    E, tm = sel_ref.shape

    def piece_copy(s, e, src_row, dst_row):
        return pltpu.make_async_copy(y_hbm.at[e, pl.ds(src_row, PIECE), :],
                                     stack.at[s, pl.ds(dst_row, PIECE), :], sem.at[s])

    def fetch(tile, s):
        segs, _ = _segments(cnt_ref, pos_ref, tile, E)
        for e in range(E):
            base, pieces, start, _, _ = segs[e]

            def issue(c, carry, base=base, start=start, e=e):
                piece_copy(s, e, pl.multiple_of(start + c * PIECE, SUB),
                           pl.multiple_of(base + c * PIECE, PIECE)).start()
                return carry

            lax.fori_loop(0, pieces, issue, 0)

    @pl.when(i == 0)
    def _():
        stack[...] = jnp.zeros_like(stack)
        fetch(0, 0)

    @pl.when(i + 1 < nt)
    def _():
        fetch(i + 1, 1 - slot)

    segs, total = _segments(cnt_ref, pos_ref, i, E)
    col = _stack_columns(segs, sel_ref[...], tm)
    aff = aff_ref[...]
    o_ref[...] = x_ref[...]
    _wait_pieces(_total_pieces(segs), piece_copy(slot, 0, 0, 0))

    def chunk(k, c):
        j0 = pl.multiple_of(k * CHUNK, CHUNK)
        jrow = (lax.broadcasted_iota(jnp.int32, (CHUNK, tm), 0) + j0).astype(F32)
        oh = jnp.zeros((CHUNK, tm), F32)
        ga = jnp.zeros((CHUNK, tm), F32)
        for e in range(E):
            hit = jrow == col[e:e + 1, :]
            oh = jnp.where(hit, 1.0, oh)
            ga = jnp.where(hit, aff[e:e + 1, :], ga)
        gate = jnp.sum(ga, axis=1, keepdims=True)
        y = stack[slot, pl.ds(j0, CHUNK), :] * gate
        hi = y.astype(BF16)
        lo = (y - hi.astype(F32)).astype(BF16)
        ohb = oh.astype(BF16)
        tn = (((0,), (0,)), ((), ()))
        o_ref[...] += (lax.dot_general(ohb, hi, tn, preferred_element_type=F32)
                       + lax.dot_general(ohb, lo, tn, preferred_element_type=F32))
        return c

    lax.fori_loop(0, _cdiv_pow2(total, LOG2_CHUNK), chunk, 0)


def _combine(sel, aff, cnt, pos, x, y):
    NT, E, tm = sel.shape
    T, D = x.shape
    tile_spec = pl.BlockSpec((None, E, tm), lambda i, c, p: (i, 0, 0))
    x_spec = pl.BlockSpec((tm, D), lambda i, c, p: (i, 0))
    return pl.pallas_call(
        _combine_kernel,
        grid_spec=pltpu.PrefetchScalarGridSpec(
            num_scalar_prefetch=2,
            grid=(NT,),
            in_specs=[tile_spec, tile_spec, x_spec, pl.BlockSpec(memory_space=pl.ANY)],
            out_specs=x_spec,
            scratch_shapes=[pltpu.VMEM((2, _stack_rows(E, tm), D), F32),
                            pltpu.SemaphoreType.DMA((2,))]),
        out_shape=jax.ShapeDtypeStruct((T, D), F32),
        compiler_params=_params(1),
        name="expert_combine",
    )(cnt, pos, sel, aff, x, y)


def _ffn_kernel(x_ref, wg_ref, wu_ref, wd_ref, y_ref):
    i = pl.program_id(1)
    last = pl.num_programs(1) - 1

    @pl.when(i < last)
    def _():
        x = x_ref[...].astype(BF16)
        a = jnp.dot(x, wg_ref[...], preferred_element_type=F32)
        b = jnp.dot(x, wu_ref[...], preferred_element_type=F32)
        h = (a * (1.0 / (1.0 + jnp.exp(-a)))) * b
        y_ref[...] = jnp.dot(h.astype(BF16), wd_ref[...], preferred_element_type=F32)

    @pl.when(i == last)
    def _():
        y_ref[...] = jnp.zeros_like(y_ref)


def _expert_ffn(xe, cap, wg, wu, wd):
    E, _, D = xe.shape
    F = wg.shape[2]
    tm = min(ROW_TILE, cap)
    n = cap // tm
    return pl.pallas_call(
        _ffn_kernel,
        grid=(E, n + 1),
        in_specs=[
            pl.BlockSpec((None, tm, D), lambda e, i: (e, jnp.minimum(i, n - 1), 0)),
            pl.BlockSpec((None, D, F), lambda e, i: (e, 0, 0)),
            pl.BlockSpec((None, D, F), lambda e, i: (e, 0, 0)),
            pl.BlockSpec((None, F, D), lambda e, i: (e, 0, 0)),
        ],
        out_specs=pl.BlockSpec((None, tm, D), lambda e, i: (e, i, 0)),
        out_shape=jax.ShapeDtypeStruct((E, cap + tm, D), F32),
        compiler_params=_params(2),
        name=f"expert_ffn_cap{cap}",
    )(xe, wg, wu, wd)


def _norm_kernel(x_ref, g_ref, y_ref):
    y_ref[...] = _rms(x_ref[...], g_ref[...])


def _final_norm(x, g):
    T, D = x.shape
    tm = ROW_TILE
    return pl.pallas_call(
        _norm_kernel,
        grid=(T // tm,),
        in_specs=[pl.BlockSpec((tm, D), lambda i: (i, 0)), pl.BlockSpec((1, D), lambda i: (0, 0))],
        out_specs=pl.BlockSpec((tm, D), lambda i: (i, 0)),
        out_shape=jax.ShapeDtypeStruct((T, D), F32),
        compiler_params=_params(1),
        name="final_norm",
    )(x, g)


def _moe(x, g, w_rt, wg, wu, wd):
    T, D = x.shape
    hn, aff = _router(x, g, w_rt)
    cap = max(1, (EC_CAPACITY * T) // N_EXPERTS)
    sel, cnt, pos = _select(aff, cap)
    xe = _dispatch(sel, cnt, pos, hn, cap)
    y = _expert_ffn(xe, cap, wg, wu, wd)
    return _combine(sel, aff, cnt, pos, x, y)


def _trunk(x3, p):
    B, S, D = x3.shape
    x = x3.reshape(B * S, D)
    tables = _rope_tables(S)
    depth = p["ffn_norm"].shape[0]
    for i in range(depth):
        j = i // 2
        if i % 2 == 0:
            qkvs = _qkv_proj(x, p["attn_norm"][j][None], p["w_qkv"][j], tables, B, S)
            res = [_dilated_attention(qkvs[g], g) for g in range(N_GROUPS)]
            x = _attn_out_proj([r[0] for r in res], [r[1] for r in res], p["w_attn_out"][j], x, S)
        else:
            x = _short_conv(x, p["conv_norm"][j][None], p["w_conv_in"][j], p["conv_w"][j],
                            p["w_conv_out"][j], S)
        x = _moe(x, p["ffn_norm"][i][None], p["w_router_t"][i], p["w_gate"][i], p["w_up"][i], p["w_down"][i])
    return _final_norm(x, p["final_norm"][None]).reshape(B, S, D)


def kernel(x_prompt, x_sample, attn_norm, w_qkv, w_attn_out, conv_norm, w_conv_in, conv_w, w_conv_out,
           ffn_norm, w_router, w_gate, w_up, w_down, final_norm):
    p = dict(
        attn_norm=attn_norm, conv_norm=conv_norm, ffn_norm=ffn_norm, final_norm=final_norm, conv_w=conv_w,
        w_qkv=w_qkv.astype(BF16), w_attn_out=w_attn_out.astype(BF16),
        w_conv_in=w_conv_in.astype(BF16), w_conv_out=w_conv_out.astype(BF16),
        w_router_t=jnp.swapaxes(w_router, 1, 2).astype(BF16),
        w_gate=w_gate.astype(BF16), w_up=w_up.astype(BF16), w_down=w_down.astype(BF16),
    )
    return (_trunk(x_prompt, p), _trunk(x_sample, p))
```

```python
import functools

import jax
import jax.numpy as jnp
from jax import lax
from jax.experimental import pallas as pl
from jax.experimental.pallas import tpu as pltpu

D_MODEL = 1024
HEAD_DIM = 64
HEADS_PER_GROUP = 4
DILATED_GROUPS = ((128, 1), (512, 4), (2048, 16))
N_GROUPS = len(DILATED_GROUPS)
ATTN_WIDTH = N_GROUPS * HEADS_PER_GROUP * HEAD_DIM
GROUP_WIDTH = HEADS_PER_GROUP * HEAD_DIM
ROPE_DIM = HEAD_DIM // 4
ROPE_THETA = 500000.0
N_EXPERTS = 16
EC_CAPACITY = 2
RMS_EPS = 1e-6
NEG_INF = -1e30

LANES = 128
SUB = 8
SUB_BF16 = 16
ROW_TILE = 512
ATTN_TQ = 128
MOE_TILE = 256
CHUNK = 256
PIECE = 16
LOG2_PIECE = 4
LOG2_CHUNK = 8
LOG2_SUB = 3
assert (1 << LOG2_PIECE, 1 << LOG2_CHUNK, 1 << LOG2_SUB) == (PIECE, CHUNK, SUB)
VMEM_LIMIT = 56 * 1024 * 1024

F32 = jnp.float32
BF16 = jnp.bfloat16


def _params(n_axes):
    return pltpu.CompilerParams(
        dimension_semantics=("arbitrary",) * n_axes, vmem_limit_bytes=VMEM_LIMIT)


def _rms(x, g):
    ms = jnp.mean(x * x, axis=-1, keepdims=True)
    return x * lax.rsqrt(ms + RMS_EPS) * g


def _qkv_kernel(x_ref, g_ref, w_ref, cos_ref, sa_ref, sb_ref, o0_ref, o1_ref, o2_ref, s_ref):
    tm = x_ref.shape[0]
    h = _rms(x_ref[...], g_ref[...]).astype(BF16)
    acc = jnp.dot(h, w_ref[...], preferred_element_type=F32)
    c, sa, sb = cos_ref[...], sa_ref[...], sb_ref[...]
    n_q = ATTN_WIDTH // LANES
    for j in range(2 * n_q):
        blk = acc[:, j * LANES:(j + 1) * LANES]
        r = blk * c + pltpu.roll(blk, LANES - ROPE_DIM // 2, 1) * sa + pltpu.roll(blk, ROPE_DIM // 2, 1) * sb
        if j < n_q:
            r = r * (HEAD_DIM ** -0.5)
        s_ref[j] = r
    for j in range(2 * n_q, 3 * n_q):
        s_ref[j] = acc[:, j * LANES:(j + 1) * LANES]
    per_group = GROUP_WIDTH // LANES
    for g, o_ref in enumerate((o0_ref, o1_ref, o2_ref)):
        dil = DILATED_GROUPS[g][1]
        for part in range(3):
            for jj in range(per_group):
                src = part * n_q + g * per_group + jj
                dst = slice((part * per_group + jj) * LANES, (part * per_group + jj + 1) * LANES)
                for r in range(dil):
                    rows = pl.ds(r, tm // dil, stride=dil) if dil > 1 else slice(None)
                    o_ref[r, :, dst] = s_ref[src, rows, :].astype(BF16)


def _rope_tables(S):
    half = ROPE_DIM // 2
    inv_freq = jnp.power(ROPE_THETA, -jnp.arange(0, ROPE_DIM, 2, dtype=F32) / ROPE_DIM)
    ang = jnp.arange(S, dtype=F32)[:, None] * inv_freq[None, :]
    cos, sin = jnp.cos(ang), jnp.sin(ang)
    rest = HEAD_DIM - ROPE_DIM
    c = jnp.concatenate([cos, cos, jnp.ones((S, rest), F32)], axis=1)
    sa = jnp.concatenate([-sin, jnp.zeros((S, half + rest), F32)], axis=1)
    sb = jnp.concatenate([jnp.zeros((S, half), F32), sin, jnp.zeros((S, rest), F32)], axis=1)
    rep = LANES // HEAD_DIM
    return tuple(jnp.tile(t, (1, rep)) for t in (c, sa, sb))


def _qkv_proj(x, g, w, tables, B, S):
    T, D = x.shape
    N = w.shape[1]
    tm = ROW_TILE
    per_seq = S // tm
    tab_spec = pl.BlockSpec((tm, LANES), lambda i: (i % per_seq, 0))
    dils = [d for _, d in DILATED_GROUPS]
    return pl.pallas_call(
        _qkv_kernel,
        grid=(T // tm,),
        in_specs=[
            pl.BlockSpec((tm, D), lambda i: (i, 0)),
            pl.BlockSpec((1, D), lambda i: (0, 0)),
            pl.BlockSpec((D, N), lambda i: (0, 0)),
            tab_spec, tab_spec, tab_spec,
        ],
        out_specs=[pl.BlockSpec((None, d, tm // d, 3 * GROUP_WIDTH), lambda i: (i // per_seq, 0, i % per_seq, 0))
                   for d in dils],
        out_shape=[jax.ShapeDtypeStruct((B, d, S // d, 3 * GROUP_WIDTH), BF16) for d in dils],
        scratch_shapes=[pltpu.VMEM((N // LANES, tm, LANES), F32)],
        compiler_params=_params(1),
        name="qkv_proj",
    )(x, g, w, *tables)


def _attn_kernel(q_ref, k_ref, v_ref, o_ref, l_ref, *, L, Lq, kw, tq, radius):
    c = pl.program_id(2)

    def body(i, carry):
        r0 = pl.multiple_of(i * tq, tq)
        t0 = c * Lq + i * tq
        ks = pl.multiple_of(jnp.clip(t0 - radius, 0, L - kw), radius)
        qb = q_ref[pl.ds(r0, tq), :]
        kb = k_ref[pl.ds(ks, kw), :]
        vb = v_ref[pl.ds(ks, kw), :]
        qi = t0 + lax.broadcasted_iota(jnp.int32, (tq, kw), 0)
        kj = ks + lax.broadcasted_iota(jnp.int32, (tq, kw), 1)
        valid = jnp.abs(qi - kj) <= radius
        outs, lses = [], []
        for h in range(HEADS_PER_GROUP):
            sl = slice(h * HEAD_DIM, (h + 1) * HEAD_DIM)
            s = lax.dot_general(qb[:, sl], kb[:, sl], (((1,), (1,)), ((), ())),
                                preferred_element_type=F32)
            s = jnp.where(valid, s, NEG_INF)
            m = jnp.max(s, axis=-1, keepdims=True)
            p = jnp.exp(s - m)
            l = jnp.sum(p, axis=-1, keepdims=True)
            o = jnp.dot(p.astype(BF16), vb[:, sl], preferred_element_type=F32) / l
            outs.append(o)
            lses.append(jnp.broadcast_to(m + jnp.log(l), (tq, HEAD_DIM)))
        o_ref[pl.ds(r0, tq), :] = jnp.concatenate(outs, axis=1)
        l_ref[pl.ds(r0, tq), :] = jnp.concatenate(lses, axis=1)
        return carry

    lax.fori_loop(0, Lq // tq, body, 0)


def _dilated_attention(qkv, group):
    window, dil = DILATED_GROUPS[group]
    B, _, L, _ = qkv.shape
    radius = window // (2 * dil)
    tq = min(ATTN_TQ, L)
    kw = min(L, tq + 2 * radius)
    Lq = min(L, 1024)
    assert L % tq == 0 and L % Lq == 0 and Lq % tq == 0 and radius % 16 == 0 and (L - kw) % radius == 0
    kern = functools.partial(_attn_kernel, L=L, Lq=Lq, kw=kw, tq=tq, radius=radius)
    out_spec = pl.BlockSpec((None, None, Lq, GROUP_WIDTH), lambda b, r, c: (b, r, c, 0))
    return pl.pallas_call(
        kern,
        grid=(B, dil, L // Lq),
        in_specs=[
            pl.BlockSpec((None, None, Lq, GROUP_WIDTH), lambda b, r, c: (b, r, c, 0)),
            pl.BlockSpec((None, None, L, GROUP_WIDTH), lambda b, r, c: (b, r, 0, 1)),
            pl.BlockSpec((None, None, L, GROUP_WIDTH), lambda b, r, c: (b, r, 0, 2)),
        ],
        out_specs=[out_spec, out_spec],
        out_shape=[jax.ShapeDtypeStruct((B, dil, L, GROUP_WIDTH), F32)] * 2,
        compiler_params=_params(3),
        name=f"dilated_attn_g{group}",
    )(qkv, qkv, qkv)


def _oproj_kernel(o0_ref, o1_ref, o2_ref, l0_ref, l1_ref, l2_ref, w_ref, x_ref, y_ref, o_scr, l_scr):
    tm = x_ref.shape[0]
    for g, (o_ref, l_ref) in enumerate(((o0_ref, l0_ref), (o1_ref, l1_ref), (o2_ref, l2_ref))):
        dil = DILATED_GROUPS[g][1]
        for r in range(dil):
            rows = pl.ds(r, tm // dil, stride=dil) if dil > 1 else slice(None)
            for jj in range(GROUP_WIDTH // LANES):
                lanes = slice(jj * LANES, (jj + 1) * LANES)
                o_scr[g, jj, rows, :] = o_ref[r, :, lanes]
                l_scr[g, jj, rows, :] = l_ref[r, :, lanes]
    zs = []
    for jj in range(GROUP_WIDTH // LANES):
        l0, l1, l2 = l_scr[0, jj], l_scr[1, jj], l_scr[2, jj]
        m = jnp.maximum(jnp.maximum(l0, l1), l2)
        e0, e1, e2 = jnp.exp(l0 - m), jnp.exp(l1 - m), jnp.exp(l2 - m)
        den = e0 + e1 + e2
        zs.append((o_scr[0, jj] * (e0 / den), o_scr[1, jj] * (e1 / den), o_scr[2, jj] * (e2 / den)))
    z = jnp.concatenate([zs[jj][g] for g in range(N_GROUPS) for jj in range(GROUP_WIDTH // LANES)], axis=1)
    y_ref[...] = x_ref[...] + jnp.dot(z.astype(BF16), w_ref[...], preferred_element_type=F32)


def _attn_out_proj(outs, lses, w, x, S):
    T, D = x.shape
    tm = ROW_TILE
    per_seq = S // tm
    gspecs = [pl.BlockSpec((None, d, tm // d, GROUP_WIDTH), lambda i: (i // per_seq, 0, i % per_seq, 0))
              for _, d in DILATED_GROUPS]
    xspec = pl.BlockSpec((tm, D), lambda i: (i, 0))
    return pl.pallas_call(
        _oproj_kernel,
        grid=(T // tm,),
        in_specs=gspecs + gspecs + [pl.BlockSpec((ATTN_WIDTH, D), lambda i: (0, 0)), xspec],
        out_specs=xspec,
        out_shape=jax.ShapeDtypeStruct((T, D), F32),
        scratch_shapes=[pltpu.VMEM((N_GROUPS, GROUP_WIDTH // LANES, tm, LANES), F32)] * 2,
        compiler_params=_params(1),
        name="attn_out_proj",
    )(*outs, *lses, w, x)


def _conv_kernel(x_ref, xp_ref, xn_ref, g_ref, win_ref, cw_ref, wout_ref, y_ref, u_scr, *, tm, per_seq):
    i = pl.program_id(0)
    D = x_ref.shape[1]
    x = x_ref[...]
    xa = jnp.concatenate([xp_ref[...], x, xn_ref[...]], axis=0)
    h = _rms(xa, g_ref[...]).astype(BF16)
    full = jnp.dot(h, win_ref[...], preferred_element_type=F32)
    u_scr[...] = full[:, D:2 * D] * full[:, 2 * D:]
    first = (i % per_seq) == 0
    last = (i % per_seq) == per_seq - 1
    u_scr[pl.ds(7, 1), :] = jnp.where(first, 0.0, u_scr[pl.ds(7, 1), :])
    u_scr[pl.ds(tm + 8, 1), :] = jnp.where(last, 0.0, u_scr[pl.ds(tm + 8, 1), :])
    cw = cw_ref[...]
    conv = (cw[0:1] * u_scr[pl.ds(7, tm), :] + cw[1:2] * u_scr[pl.ds(8, tm), :]
            + cw[2:3] * u_scr[pl.ds(9, tm), :])
    z = (full[8:8 + tm, :D] * conv).astype(BF16)
    y_ref[...] = x + jnp.dot(z, wout_ref[...], preferred_element_type=F32)


def _short_conv(x, g, w_in, cw, w_out, S):
    T, D = x.shape
    tm = ROW_TILE
    per_seq = S // tm
    sub = tm // 8
    n8 = T // 8
    kern = functools.partial(_conv_kernel, tm=tm, per_seq=per_seq)
    return pl.pallas_call(
        kern,
        grid=(T // tm,),
        in_specs=[
            pl.BlockSpec((tm, D), lambda i: (i, 0)),
            pl.BlockSpec((8, D), lambda i: (jnp.maximum(i * sub - 1, 0), 0)),
            pl.BlockSpec((8, D), lambda i: (jnp.minimum((i + 1) * sub, n8 - 1), 0)),
            pl.BlockSpec((1, D), lambda i: (0, 0)),
            pl.BlockSpec((D, 3 * D), lambda i: (0, 0)),
            pl.BlockSpec((3, D), lambda i: (0, 0)),
            pl.BlockSpec((D, D), lambda i: (0, 0)),
        ],
        out_specs=pl.BlockSpec((tm, D), lambda i: (i, 0)),
        out_shape=jax.ShapeDtypeStruct((T, D), F32),
        scratch_shapes=[pltpu.VMEM((tm + 16, D), F32)],
        compiler_params=_params(1),
        name="short_conv",
    )(x, x, x, g, w_in, cw, w_out)


def _router_kernel(x_ref, g_ref, wr_ref, hn_ref, aff_ref):
    h = _rms(x_ref[...], g_ref[...]).astype(BF16)
    hn_ref[...] = h
    logits = lax.dot_general(wr_ref[...], h, (((1,), (1,)), ((), ())),
                             preferred_element_type=F32)
    m = jnp.max(logits, axis=0, keepdims=True)
    e = jnp.exp(logits - m)
    aff = e / jnp.sum(e, axis=0, keepdims=True)
    for k in range(aff_ref.shape[0]):
        aff_ref[k] = aff[:, k * MOE_TILE:(k + 1) * MOE_TILE]


def _router(x, g, w_rt):
    T, D = x.shape
    E = w_rt.shape[0]
    tm = ROW_TILE
    per = tm // MOE_TILE
    return pl.pallas_call(
        _router_kernel,
        grid=(T // tm,),
        in_specs=[
            pl.BlockSpec((tm, D), lambda i: (i, 0)),
            pl.BlockSpec((1, D), lambda i: (0, 0)),
            pl.BlockSpec((E, D), lambda i: (0, 0)),
        ],
        out_specs=[pl.BlockSpec((tm, D), lambda i: (i, 0)),
                   pl.BlockSpec((per, E, MOE_TILE), lambda i: (i, 0, 0))],
        out_shape=[jax.ShapeDtypeStruct((T, D), BF16),
                   jax.ShapeDtypeStruct((T // MOE_TILE, E, MOE_TILE), F32)],
        compiler_params=_params(1),
        name="router",
    )(x, g, w_rt)


def _strict_upper(n):
    r = lax.broadcasted_iota(jnp.int32, (n, n), 0)
    c = lax.broadcasted_iota(jnp.int32, (n, n), 1)
    return (r < c).astype(BF16)


def _select_kernel(aff_ref, sel_ref, cnt_ref, pos_ref, *, cap):
    NT, E, tm = aff_ref.shape
    capf = float(cap)

    def count_ge(v):
        bits = lax.bitcast_convert_type(aff_ref[...], jnp.int32)
        ge = (bits >= v[None]).astype(F32)
        return jnp.sum(jnp.sum(ge, axis=0), axis=1, keepdims=True)

    def bisect(_, lh):
        lo, hi = lh
        mid = lo + lax.shift_right_logical(hi - lo, 1)
        ok = count_ge(mid) >= capf
        return jnp.where(ok, mid, lo), jnp.where(ok, hi, mid)

    one_bits = 0x3F800000
    lo0 = jnp.zeros((E, 1), jnp.int32)
    hi0 = jnp.full((E, 1), one_bits + 1, jnp.int32)
    tau, _ = lax.fori_loop(0, 31, bisect, (lo0, hi0))
    need = capf - count_ge(tau + 1)
    tri = _strict_upper(tm)

    def tile(j, carry):
        ceq, csel = carry
        bits = lax.bitcast_convert_type(aff_ref[j], jnp.int32)
        gt = bits > tau
        eq = (bits == tau).astype(F32)
        rank_eq = jnp.dot(eq.astype(BF16), tri, preferred_element_type=F32) + ceq
        sel = jnp.where(gt | ((eq > 0) & (rank_eq < need)), 1.0, 0.0)
        sel_ref[j] = sel
        c = jnp.sum(sel, axis=1, keepdims=True)
        cnt_ref[j] = jnp.broadcast_to(c, (E, LANES))
        pos_ref[j] = jnp.broadcast_to(csel, (E, LANES))
        return ceq + jnp.sum(eq, axis=1, keepdims=True), csel + c

    zero = jnp.zeros((E, 1), F32)
    lax.fori_loop(0, NT, tile, (zero, zero))


def _select(aff, cap):
    NT, E, tm = aff.shape
    sel, cnt, pos = pl.pallas_call(
        functools.partial(_select_kernel, cap=cap),
        out_shape=[jax.ShapeDtypeStruct((NT, E, tm), F32),
                   jax.ShapeDtypeStruct((NT, E, LANES), F32),
                   jax.ShapeDtypeStruct((NT, E, LANES), F32)],
        compiler_params=pltpu.CompilerParams(vmem_limit_bytes=VMEM_LIMIT),
        name="expert_select",
    )(aff)
    to_smem = lambda a: a[:, :, 0].astype(jnp.int32).reshape(NT * E)
    return sel, to_smem(cnt), to_smem(pos)


def _stack_rows(E, tm, sub):
    worst = E * tm + E * (sub - 1 + PIECE - 1)
    return -(-worst // CHUNK) * CHUNK


def _cdiv_pow2(x, log2):
    return lax.shift_right_logical(x + ((1 << log2) - 1), log2)


def _segments(cnt_ref, pos_ref, tile, E, sub):
    segs = []
    base = jnp.int32(0)
    for e in range(E):
        n = cnt_ref[tile * E + e]
        p0 = pos_ref[tile * E + e]
        a = p0 & (sub - 1)
        pieces = jnp.where(n > 0, _cdiv_pow2(a + n, LOG2_PIECE), 0)
        segs.append((base, pieces, p0 - a, a, n))
        base = base + pieces * PIECE
    return segs, base


NO_RANK = -256.0
NO_OWNER = 1024.0
OWNER_COPIES = 3


def _segment_rows(segs):
    E = len(segs)
    lane = lax.broadcasted_iota(jnp.int32, (1, LANES), 1)
    used = lane < OWNER_COPIES * E
    first = jnp.full((1, LANES), float(NO_OWNER * NO_OWNER), F32)
    end = jnp.full((1, LANES), -1.0, F32)
    zero = jnp.zeros((1, LANES), F32)
    for e, (base, pieces, _, a, _) in enumerate(segs):
        m = used & ((lane & (E - 1)) == e)
        first = jnp.where(m, base.astype(F32), first)
        end = jnp.where(m, (base + pieces * PIECE).astype(F32), end)
        zero = jnp.where(m, (base + a).astype(F32), zero)
    return first, end, zero


def _tile_ranks(sel, tm):
    E = sel.shape[0]
    rank = jnp.dot(sel.astype(BF16), _strict_upper(tm), preferred_element_type=F32)
    rank = jnp.where(sel > 0, rank, NO_RANK).astype(BF16)
    return jnp.concatenate([rank, jnp.zeros((LANES - E, tm), BF16)], axis=0)


def _chunk_one_hot(j0, rows, ranks, E):
    first, end, zero = rows
    jv = (j0 + lax.broadcasted_iota(jnp.int32, (CHUNK, LANES), 0)).astype(F32)
    own = (jv >= first) & (jv < end)
    lane = lax.broadcasted_iota(jnp.int32, (CHUNK, LANES), 1)
    tgt = jnp.sum(jnp.where(own & (lane < E), jv - zero + NO_OWNER, 0.0), axis=1, keepdims=True) - NO_OWNER
    ownb = jnp.where(own, 1.0, 0.0).astype(BF16)
    hit = jnp.dot(ownb, ranks, preferred_element_type=F32) == tgt
    return ownb, hit


def _total_pieces(segs):
    t = segs[0][1]
    for s in segs[1:]:
        t = t + s[1]
    return t


def _wait_pieces(n, piece_copy):
    def w(_, c):
        piece_copy.wait()
        return c
    lax.fori_loop(0, n, w, 0)


def _dispatch_kernel(cnt_ref, pos_ref, sel_ref, hn_ref, xe_ref, stack, pend, sem):
    i = pl.program_id(0)
    nt = pl.num_programs(0)
    slot = i % 2
    E, tm = sel_ref.shape

    @pl.when(i == 0)
    def _():
        pend[...] = jnp.zeros_like(pend)

    segs, total = _segments(cnt_ref, pos_ref, i, E, SUB)
    rows = _segment_rows(segs)
    ranks = _tile_ranks(sel_ref[...], tm)

    def chunk(k, c):
        j0 = pl.multiple_of(k * CHUNK, CHUNK)
        _, hit = _chunk_one_hot(j0, rows, ranks, E)
        oh = jnp.where(hit, 1.0, 0.0).astype(BF16)
        stack[slot, pl.ds(j0, CHUNK), :] = jnp.dot(oh, hn_ref[...], preferred_element_type=F32)
        return c

    lax.fori_loop(0, _cdiv_pow2(total, LOG2_CHUNK), chunk, 0)

    for e in range(E):
        base, _, _, a, n = segs[e]

        @pl.when(n > 0)
        def _():
            first = pl.multiple_of(base, PIECE)
            stack[slot, pl.ds(first, SUB), :] = stack[slot, pl.ds(first, SUB), :] + pend[e]
            last = pl.multiple_of(base + lax.shift_right_logical(a + n - 1, LOG2_SUB) * SUB, SUB)
            incomplete = ((a + n) & (SUB - 1)) != 0
            pend[e] = jnp.where(incomplete, stack[slot, pl.ds(last, SUB), :], 0.0)

    def piece_copy(s, src_row, e, dst_row):
        return pltpu.make_async_copy(stack.at[s, pl.ds(src_row, PIECE), :],
                                     xe_ref.at[e, pl.ds(dst_row, PIECE), :], sem.at[s])

    @pl.when(i > 0)
    def _():
        prev, _ = _segments(cnt_ref, pos_ref, i - 1, E, SUB)
        _wait_pieces(_total_pieces(prev), piece_copy(1 - slot, 0, 0, 0))

    for e in range(E):
        base, pieces, start, _, _ = segs[e]

        def issue(c, carry, base=base, start=start, e=e):
            piece_copy(slot, pl.multiple_of(base + c * PIECE, PIECE), e,
                       pl.multiple_of(start + c * PIECE, SUB)).start()
            return carry

        lax.fori_loop(0, pieces, issue, 0)

    @pl.when(i == nt - 1)
    def _():
        _wait_pieces(_total_pieces(segs), piece_copy(slot, 0, 0, 0))
        cap = xe_ref.shape[1] - PIECE
        stack[slot, pl.ds(0, PIECE), :] = jnp.zeros((PIECE, stack.shape[2]), F32)
        for e in range(E):
            piece_copy(slot, 0, e, cap).start()
        _wait_pieces(E, piece_copy(slot, 0, 0, 0))


def _dispatch(sel, cnt, pos, hn, cap):
    NT, E, tm = sel.shape
    T, D = hn.shape
    return pl.pallas_call(
        _dispatch_kernel,
        grid_spec=pltpu.PrefetchScalarGridSpec(
            num_scalar_prefetch=2,
            grid=(NT,),
            in_specs=[pl.BlockSpec((None, E, tm), lambda i, c, p: (i, 0, 0)),
                      pl.BlockSpec((tm, D), lambda i, c, p: (i, 0))],
            out_specs=pl.BlockSpec(memory_space=pl.ANY),
            scratch_shapes=[pltpu.VMEM((2, _stack_rows(E, tm, SUB), D), F32),
                            pltpu.VMEM((E, SUB, D), F32),
                            pltpu.SemaphoreType.DMA((2,))]),
        out_shape=jax.ShapeDtypeStruct((E, cap + PIECE, D), F32),
        compiler_params=_params(1),
        name="expert_dispatch",
    )(cnt, pos, sel, hn)


def _segment_cols(segs):
    E = len(segs)
    eidx = lax.broadcasted_iota(jnp.int32, (E, 1), 0)
    first = jnp.zeros((E, 1), F32)
    end = jnp.zeros((E, 1), F32)
    zero = jnp.zeros((E, 1), F32)
    for e, (base, pieces, _, a, _) in enumerate(segs):
        m = eidx == e
        first = jnp.where(m, base.astype(F32), first)
        end = jnp.where(m, (base + pieces * PIECE).astype(F32), end)
        zero = jnp.where(m, (base + a).astype(F32), zero)
    return first, end, zero


def _combine_kernel(cnt_ref, pos_ref, sel_ref, aff_ref, x_ref, y_hbm, o_ref, stack, sem):
    i = pl.program_id(0)
    nt = pl.num_programs(0)
    slot = i % 2
    E, tm = sel_ref.shape

    def piece_copy(s, e, src_row, dst_row):
        return pltpu.make_async_copy(y_hbm.at[e, pl.ds(src_row, PIECE), :],
                                     stack.at[s, pl.ds(dst_row, PIECE), :], sem.at[s])

    def fetch(tile, s):
        segs, _ = _segments(cnt_ref, pos_ref, tile, E, SUB_BF16)
        for e in range(E):
            base, pieces, start, _, _ = segs[e]

            def issue(c, carry, base=base, start=start, e=e):
                piece_copy(s, e, pl.multiple_of(start + c * PIECE, SUB_BF16),
                           pl.multiple_of(base + c * PIECE, PIECE)).start()
                return carry

            lax.fori_loop(0, pieces, issue, 0)

    @pl.when(i == 0)
    def _():
        stack[...] = jnp.zeros_like(stack)
        fetch(0, 0)

    @pl.when(i + 1 < nt)
    def _():
        fetch(i + 1, 1 - slot)

    segs, total = _segments(cnt_ref, pos_ref, i, E, SUB_BF16)
    first, end, zero = _segment_cols(segs)
    sel = sel_ref[...]
    aff = aff_ref[...]
    a_hi = aff.astype(BF16).astype(F32)
    a_mid = (aff - a_hi).astype(BF16).astype(F32)
    a_lo = ((aff - a_hi) - a_mid).astype(BF16).astype(F32)
    rank = jnp.dot(sel.astype(BF16), _strict_upper(tm), preferred_element_type=F32)
    rank = jnp.where(sel > 0, rank, NO_RANK)
    pad = jnp.zeros((LANES - (OWNER_COPIES + 1) * E, tm), F32)
    by_token = jnp.transpose(jnp.concatenate([a_hi, a_mid, a_lo, rank, pad], axis=0)).astype(BF16)
    o_ref[...] = x_ref[...]
    _wait_pieces(_total_pieces(segs), piece_copy(slot, 0, 0, 0))

    def chunk(k, c):
        j0 = pl.multiple_of(k * CHUNK, CHUNK)
        jl = (j0 + lax.broadcasted_iota(jnp.int32, (E, CHUNK), 1)).astype(F32)
        own = (jl >= first) & (jl < end)
        tgt = jnp.sum(jnp.where(own, jl - zero + NO_OWNER, 0.0), axis=0, keepdims=True) - NO_OWNER
        o16 = jnp.where(own, 1.0, 0.0).astype(BF16)
        zeros = lambda n: jnp.zeros((n, CHUNK), BF16)
        own_aff = jnp.concatenate([o16] * OWNER_COPIES + [zeros(LANES - OWNER_COPIES * E)], axis=0)
        own_rank = jnp.concatenate([zeros(OWNER_COPIES * E), o16, zeros(LANES - (OWNER_COPIES + 1) * E)], axis=0)
        hit = jnp.dot(by_token, own_rank, preferred_element_type=F32) == tgt
        g = jnp.where(hit, jnp.dot(by_token, own_aff, preferred_element_type=F32), 0.0)
        g_hi = g.astype(BF16)
        g_lo = (g - g_hi.astype(F32)).astype(BF16)
        y = stack[slot, pl.ds(j0, CHUNK), :]
        o_ref[...] += (jnp.dot(g_hi, y, preferred_element_type=F32)
                       + jnp.dot(g_lo, y, preferred_element_type=F32))
        return c

    lax.fori_loop(0, _cdiv_pow2(total, LOG2_CHUNK), chunk, 0)


def _combine(sel, aff, cnt, pos, x, y):
    NT, E, tm = sel.shape
    T, D = x.shape
    tile_spec = pl.BlockSpec((None, E, tm), lambda i, c, p: (i, 0, 0))
    x_spec = pl.BlockSpec((tm, D), lambda i, c, p: (i, 0))
    return pl.pallas_call(
        _combine_kernel,
        grid_spec=pltpu.PrefetchScalarGridSpec(
            num_scalar_prefetch=2,
            grid=(NT,),
            in_specs=[tile_spec, tile_spec, x_spec, pl.BlockSpec(memory_space=pl.ANY)],
            out_specs=x_spec,
            scratch_shapes=[pltpu.VMEM((2, _stack_rows(E, tm, SUB_BF16), D), BF16),
                            pltpu.SemaphoreType.DMA((2,))]),
        out_shape=jax.ShapeDtypeStruct((T, D), F32),
        compiler_params=_params(1),
        name="expert_combine",
    )(cnt, pos, sel, aff, x, y)


def _ffn_kernel(x_ref, wg_ref, wu_ref, wd_ref, y_ref):
    i = pl.program_id(1)
    last = pl.num_programs(1) - 1

    @pl.when(i < last)
    def _():
        x = x_ref[...].astype(BF16)
        a = jnp.dot(x, wg_ref[...], preferred_element_type=F32)
        b = jnp.dot(x, wu_ref[...], preferred_element_type=F32)
        h = (a * (1.0 / (1.0 + jnp.exp(-a)))) * b
        y_ref[...] = jnp.dot(h.astype(BF16), wd_ref[...], preferred_element_type=F32).astype(y_ref.dtype)

    @pl.when(i == last)
    def _():
        y_ref[...] = jnp.zeros_like(y_ref)


def _expert_ffn(xe, cap, wg, wu, wd):
    E, _, D = xe.shape
    F = wg.shape[2]
    tm = min(ROW_TILE, cap)
    n = cap // tm
    return pl.pallas_call(
        _ffn_kernel,
        grid=(E, n + 1),
        in_specs=[
            pl.BlockSpec((None, tm, D), lambda e, i: (e, jnp.minimum(i, n - 1), 0)),
            pl.BlockSpec((None, D, F), lambda e, i: (e, 0, 0)),
            pl.BlockSpec((None, D, F), lambda e, i: (e, 0, 0)),
            pl.BlockSpec((None, F, D), lambda e, i: (e, 0, 0)),
        ],
        out_specs=pl.BlockSpec((None, tm, D), lambda e, i: (e, i, 0)),
        out_shape=jax.ShapeDtypeStruct((E, cap + tm, D), BF16),
        compiler_params=_params(2),
        name=f"expert_ffn_cap{cap}",
    )(xe, wg, wu, wd)


def _norm_kernel(x_ref, g_ref, y_ref):
    y_ref[...] = _rms(x_ref[...], g_ref[...])


def _final_norm(x, g):
    T, D = x.shape
    tm = ROW_TILE
    return pl.pallas_call(
        _norm_kernel,
        grid=(T // tm,),
        in_specs=[pl.BlockSpec((tm, D), lambda i: (i, 0)), pl.BlockSpec((1, D), lambda i: (0, 0))],
        out_specs=pl.BlockSpec((tm, D), lambda i: (i, 0)),
        out_shape=jax.ShapeDtypeStruct((T, D), F32),
        compiler_params=_params(1),
        name="final_norm",
    )(x, g)


def _moe(x, g, w_rt, wg, wu, wd):
    T, D = x.shape
    hn, aff = _router(x, g, w_rt)
    cap = max(1, (EC_CAPACITY * T) // N_EXPERTS)
    sel, cnt, pos = _select(aff, cap)
    xe = _dispatch(sel, cnt, pos, hn, cap)
    y = _expert_ffn(xe, cap, wg, wu, wd)
    return _combine(sel, aff, cnt, pos, x, y)


def _trunk(x3, p):
    B, S, D = x3.shape
    x = x3.reshape(B * S, D)
    tables = _rope_tables(S)
    depth = p["ffn_norm"].shape[0]
    for i in range(depth):
        j = i // 2
        if i % 2 == 0:
            qkvs = _qkv_proj(x, p["attn_norm"][j][None], p["w_qkv"][j], tables, B, S)
            res = [_dilated_attention(qkvs[g], g) for g in range(N_GROUPS)]
            x = _attn_out_proj([r[0] for r in res], [r[1] for r in res], p["w_attn_out"][j], x, S)
        else:
            x = _short_conv(x, p["conv_norm"][j][None], p["w_conv_in"][j], p["conv_w"][j],
                            p["w_conv_out"][j], S)
        x = _moe(x, p["ffn_norm"][i][None], p["w_router_t"][i], p["w_gate"][i], p["w_up"][i], p["w_down"][i])
    return _final_norm(x, p["final_norm"][None]).reshape(B, S, D)


def kernel(x_prompt, x_sample, attn_norm, w_qkv, w_attn_out, conv_norm, w_conv_in, conv_w, w_conv_out,
           ffn_norm, w_router, w_gate, w_up, w_down, final_norm):
    per_layer = lambda w: [w[i].astype(BF16) for i in range(w.shape[0])]
    p = dict(
        attn_norm=attn_norm, conv_norm=conv_norm, ffn_norm=ffn_norm, final_norm=final_norm, conv_w=conv_w,
        w_qkv=per_layer(w_qkv), w_attn_out=per_layer(w_attn_out),
        w_conv_in=per_layer(w_conv_in), w_conv_out=per_layer(w_conv_out),
        w_router_t=per_layer(jnp.swapaxes(w_router, 1, 2)),
        w_gate=per_layer(w_gate), w_up=per_layer(w_up), w_down=per_layer(w_down),
    )
    return (_trunk(x_prompt, p), _trunk(x_sample, p))
```

```python
import functools

import jax
import jax.numpy as jnp
from jax import lax
from jax.experimental import pallas as pl
from jax.experimental.pallas import tpu as pltpu

D_MODEL = 1024
HEAD_DIM = 64
HEADS_PER_GROUP = 4
DILATED_GROUPS = ((128, 1), (512, 4), (2048, 16))
N_GROUPS = len(DILATED_GROUPS)
ATTN_WIDTH = N_GROUPS * HEADS_PER_GROUP * HEAD_DIM
GROUP_WIDTH = HEADS_PER_GROUP * HEAD_DIM
ROPE_DIM = HEAD_DIM // 4
ROPE_THETA = 500000.0
N_EXPERTS = 16
EC_CAPACITY = 2
RMS_EPS = 1e-6
NEG_INF = -1e30

LANES = 128
SUB = 8
SUB_BF16 = 16
ROW_TILE = 512
ATTN_TQ = 128
MOE_TILE = 256
CHUNK = 256
PIECE = 16
LOG2_PIECE = 4
LOG2_CHUNK = 8
LOG2_SUB = 3
assert (1 << LOG2_PIECE, 1 << LOG2_CHUNK, 1 << LOG2_SUB) == (PIECE, CHUNK, SUB)
VMEM_LIMIT = 56 * 1024 * 1024

F32 = jnp.float32
BF16 = jnp.bfloat16


def _params(n_axes):
    return pltpu.CompilerParams(
        dimension_semantics=("arbitrary",) * n_axes, vmem_limit_bytes=VMEM_LIMIT)


def _rms(x, g):
    ms = jnp.mean(x * x, axis=-1, keepdims=True)
    return x * lax.rsqrt(ms + RMS_EPS) * g


def _qkv_kernel(x_ref, g_ref, w_ref, cos_ref, sa_ref, sb_ref, o0_ref, o1_ref, o2_ref, s_ref):
    tm = x_ref.shape[0]
    h = _rms(x_ref[...], g_ref[...]).astype(BF16)
    acc = jnp.dot(h, w_ref[...], preferred_element_type=F32)
    c, sa, sb = cos_ref[...], sa_ref[...], sb_ref[...]
    n_q = ATTN_WIDTH // LANES
    for j in range(2 * n_q):
        blk = acc[:, j * LANES:(j + 1) * LANES]
        r = blk * c + pltpu.roll(blk, LANES - ROPE_DIM // 2, 1) * sa + pltpu.roll(blk, ROPE_DIM // 2, 1) * sb
        if j < n_q:
            r = r * (HEAD_DIM ** -0.5)
        s_ref[j] = r
    for j in range(2 * n_q, 3 * n_q):
        s_ref[j] = acc[:, j * LANES:(j + 1) * LANES]
    per_group = GROUP_WIDTH // LANES
    for g, o_ref in enumerate((o0_ref, o1_ref, o2_ref)):
        dil = DILATED_GROUPS[g][1]
        for part in range(3):
            for jj in range(per_group):
                src = part * n_q + g * per_group + jj
                dst = slice((part * per_group + jj) * LANES, (part * per_group + jj + 1) * LANES)
                for r in range(dil):
                    rows = pl.ds(r, tm // dil, stride=dil) if dil > 1 else slice(None)
                    o_ref[r, :, dst] = s_ref[src, rows, :].astype(BF16)


def _rope_tables(S):
    half = ROPE_DIM // 2
    inv_freq = jnp.power(ROPE_THETA, -jnp.arange(0, ROPE_DIM, 2, dtype=F32) / ROPE_DIM)
    ang = jnp.arange(S, dtype=F32)[:, None] * inv_freq[None, :]
    cos, sin = jnp.cos(ang), jnp.sin(ang)
    rest = HEAD_DIM - ROPE_DIM
    c = jnp.concatenate([cos, cos, jnp.ones((S, rest), F32)], axis=1)
    sa = jnp.concatenate([-sin, jnp.zeros((S, half + rest), F32)], axis=1)
    sb = jnp.concatenate([jnp.zeros((S, half), F32), sin, jnp.zeros((S, rest), F32)], axis=1)
    rep = LANES // HEAD_DIM
    return tuple(jnp.tile(t, (1, rep)) for t in (c, sa, sb))


def _qkv_proj(x, g, w, tables, B, S):
    T, D = x.shape
    N = w.shape[1]
    tm = ROW_TILE
    per_seq = S // tm
    tab_spec = pl.BlockSpec((tm, LANES), lambda i: (i % per_seq, 0))
    dils = [d for _, d in DILATED_GROUPS]
    return pl.pallas_call(
        _qkv_kernel,
        grid=(T // tm,),
        in_specs=[
            pl.BlockSpec((tm, D), lambda i: (i, 0)),
            pl.BlockSpec((1, D), lambda i: (0, 0)),
            pl.BlockSpec((D, N), lambda i: (0, 0)),
            tab_spec, tab_spec, tab_spec,
        ],
        out_specs=[pl.BlockSpec((None, d, tm // d, 3 * GROUP_WIDTH), lambda i: (i // per_seq, 0, i % per_seq, 0))
                   for d in dils],
        out_shape=[jax.ShapeDtypeStruct((B, d, S // d, 3 * GROUP_WIDTH), BF16) for d in dils],
        scratch_shapes=[pltpu.VMEM((N // LANES, tm, LANES), F32)],
        compiler_params=_params(1),
        name="qkv_proj",
    )(x, g, w, *tables)


def _attn_kernel(q_ref, k_ref, v_ref, o_ref, l_ref, *, L, Lq, kw, tq, radius):
    c = pl.program_id(2)

    def body(i, carry):
        r0 = pl.multiple_of(i * tq, tq)
        t0 = c * Lq + i * tq
        ks = pl.multiple_of(jnp.clip(t0 - radius, 0, L - kw), radius)
        qb = q_ref[pl.ds(r0, tq), :]
        kb = k_ref[pl.ds(ks, kw), :]
        vb = v_ref[pl.ds(ks, kw), :]
        qi = t0 + lax.broadcasted_iota(jnp.int32, (tq, kw), 0)
        kj = ks + lax.broadcasted_iota(jnp.int32, (tq, kw), 1)
        valid = jnp.abs(qi - kj) <= radius
        outs, lses = [], []
        for h in range(HEADS_PER_GROUP):
            sl = slice(h * HEAD_DIM, (h + 1) * HEAD_DIM)
            s = lax.dot_general(qb[:, sl], kb[:, sl], (((1,), (1,)), ((), ())),
                                preferred_element_type=F32)
            s = jnp.where(valid, s, NEG_INF)
            m = jnp.max(s, axis=-1, keepdims=True)
            p = jnp.exp(s - m)
            l = jnp.sum(p, axis=-1, keepdims=True)
            o = jnp.dot(p.astype(BF16), vb[:, sl], preferred_element_type=F32) / l
            outs.append(o)
            lses.append(jnp.broadcast_to(m + jnp.log(l), (tq, HEAD_DIM)))
        o_ref[pl.ds(r0, tq), :] = jnp.concatenate(outs, axis=1)
        l_ref[pl.ds(r0, tq), :] = jnp.concatenate(lses, axis=1)
        return carry

    n_blk = Lq // tq
    lax.fori_loop(0, n_blk, body, 0, unroll=2 if n_blk % 2 == 0 else 1)


def _dilated_attention(qkv, group):
    window, dil = DILATED_GROUPS[group]
    B, _, L, _ = qkv.shape
    radius = window // (2 * dil)
    tq = min(ATTN_TQ, L)
    kw = min(L, tq + 2 * radius)
    Lq = min(L, 1024)
    assert L % tq == 0 and L % Lq == 0 and Lq % tq == 0 and radius % 16 == 0 and (L - kw) % radius == 0
    kern = functools.partial(_attn_kernel, L=L, Lq=Lq, kw=kw, tq=tq, radius=radius)
    out_spec = pl.BlockSpec((None, None, Lq, GROUP_WIDTH), lambda b, r, c: (b, r, c, 0))
    return pl.pallas_call(
        kern,
        grid=(B, dil, L // Lq),
        in_specs=[
            pl.BlockSpec((None, None, Lq, GROUP_WIDTH), lambda b, r, c: (b, r, c, 0)),
            pl.BlockSpec((None, None, L, GROUP_WIDTH), lambda b, r, c: (b, r, 0, 1)),
            pl.BlockSpec((None, None, L, GROUP_WIDTH), lambda b, r, c: (b, r, 0, 2)),
        ],
        out_specs=[out_spec, out_spec],
        out_shape=[jax.ShapeDtypeStruct((B, dil, L, GROUP_WIDTH), F32)] * 2,
        compiler_params=_params(3),
        name=f"dilated_attn_g{group}",
    )(qkv, qkv, qkv)


def _oproj_kernel(o0_ref, o1_ref, o2_ref, l0_ref, l1_ref, l2_ref, w_ref, x_ref, y_ref, o_scr, l_scr):
    tm = x_ref.shape[0]
    for g, (o_ref, l_ref) in enumerate(((o0_ref, l0_ref), (o1_ref, l1_ref), (o2_ref, l2_ref))):
        dil = DILATED_GROUPS[g][1]
        for r in range(dil):
            rows = pl.ds(r, tm // dil, stride=dil) if dil > 1 else slice(None)
            for jj in range(GROUP_WIDTH // LANES):
                lanes = slice(jj * LANES, (jj + 1) * LANES)
                o_scr[g, jj, rows, :] = o_ref[r, :, lanes]
                l_scr[g, jj, rows, :] = l_ref[r, :, lanes]
    zs = []
    for jj in range(GROUP_WIDTH // LANES):
        l0, l1, l2 = l_scr[0, jj], l_scr[1, jj], l_scr[2, jj]
        m = jnp.maximum(jnp.maximum(l0, l1), l2)
        e0, e1, e2 = jnp.exp(l0 - m), jnp.exp(l1 - m), jnp.exp(l2 - m)
        den = e0 + e1 + e2
        zs.append((o_scr[0, jj] * (e0 / den), o_scr[1, jj] * (e1 / den), o_scr[2, jj] * (e2 / den)))
    z = jnp.concatenate([zs[jj][g] for g in range(N_GROUPS) for jj in range(GROUP_WIDTH // LANES)], axis=1)
    y_ref[...] = x_ref[...] + jnp.dot(z.astype(BF16), w_ref[...], preferred_element_type=F32)


def _attn_out_proj(outs, lses, w, x, S):
    T, D = x.shape
    tm = ROW_TILE
    per_seq = S // tm
    gspecs = [pl.BlockSpec((None, d, tm // d, GROUP_WIDTH), lambda i: (i // per_seq, 0, i % per_seq, 0))
              for _, d in DILATED_GROUPS]
    xspec = pl.BlockSpec((tm, D), lambda i: (i, 0))
    return pl.pallas_call(
        _oproj_kernel,
        grid=(T // tm,),
        in_specs=gspecs + gspecs + [pl.BlockSpec((ATTN_WIDTH, D), lambda i: (0, 0)), xspec],
        out_specs=xspec,
        out_shape=jax.ShapeDtypeStruct((T, D), F32),
        scratch_shapes=[pltpu.VMEM((N_GROUPS, GROUP_WIDTH // LANES, tm, LANES), F32)] * 2,
        compiler_params=_params(1),
        name="attn_out_proj",
    )(*outs, *lses, w, x)


def _conv_kernel(x_ref, xp_ref, xn_ref, g_ref, win_ref, cw_ref, wout_ref, y_ref, u_scr, *, tm, per_seq):
    i = pl.program_id(0)
    D = x_ref.shape[1]
    x = x_ref[...]
    xa = jnp.concatenate([xp_ref[...], x, xn_ref[...]], axis=0)
    h = _rms(xa, g_ref[...]).astype(BF16)
    full = jnp.dot(h, win_ref[...], preferred_element_type=F32)
    u_scr[...] = full[:, D:2 * D] * full[:, 2 * D:]
    first = (i % per_seq) == 0
    last = (i % per_seq) == per_seq - 1
    u_scr[pl.ds(7, 1), :] = jnp.where(first, 0.0, u_scr[pl.ds(7, 1), :])
    u_scr[pl.ds(tm + 8, 1), :] = jnp.where(last, 0.0, u_scr[pl.ds(tm + 8, 1), :])
    cw = cw_ref[...]
    conv = (cw[0:1] * u_scr[pl.ds(7, tm), :] + cw[1:2] * u_scr[pl.ds(8, tm), :]
            + cw[2:3] * u_scr[pl.ds(9, tm), :])
    z = (full[8:8 + tm, :D] * conv).astype(BF16)
    y_ref[...] = x + jnp.dot(z, wout_ref[...], preferred_element_type=F32)


def _short_conv(x, g, w_in, cw, w_out, S):
    T, D = x.shape
    tm = ROW_TILE
    per_seq = S // tm
    sub = tm // 8
    n8 = T // 8
    kern = functools.partial(_conv_kernel, tm=tm, per_seq=per_seq)
    return pl.pallas_call(
        kern,
        grid=(T // tm,),
        in_specs=[
            pl.BlockSpec((tm, D), lambda i: (i, 0)),
            pl.BlockSpec((8, D), lambda i: (jnp.maximum(i * sub - 1, 0), 0)),
            pl.BlockSpec((8, D), lambda i: (jnp.minimum((i + 1) * sub, n8 - 1), 0)),
            pl.BlockSpec((1, D), lambda i: (0, 0)),
            pl.BlockSpec((D, 3 * D), lambda i: (0, 0)),
            pl.BlockSpec((3, D), lambda i: (0, 0)),
            pl.BlockSpec((D, D), lambda i: (0, 0)),
        ],
        out_specs=pl.BlockSpec((tm, D), lambda i: (i, 0)),
        out_shape=jax.ShapeDtypeStruct((T, D), F32),
        scratch_shapes=[pltpu.VMEM((tm + 16, D), F32)],
        compiler_params=_params(1),
        name="short_conv",
    )(x, x, x, g, w_in, cw, w_out)


def _router_kernel(x_ref, g_ref, wr_ref, hn_ref, aff_ref):
    h = _rms(x_ref[...], g_ref[...]).astype(BF16)
    hn_ref[...] = h
    logits = lax.dot_general(wr_ref[...], h, (((1,), (1,)), ((), ())),
                             preferred_element_type=F32)
    m = jnp.max(logits, axis=0, keepdims=True)
    e = jnp.exp(logits - m)
    aff = e / jnp.sum(e, axis=0, keepdims=True)
    for k in range(aff_ref.shape[0]):
        aff_ref[k] = aff[:, k * MOE_TILE:(k + 1) * MOE_TILE]


def _router(x, g, w_rt):
    T, D = x.shape
    E = w_rt.shape[0]
    tm = ROW_TILE
    per = tm // MOE_TILE
    return pl.pallas_call(
        _router_kernel,
        grid=(T // tm,),
        in_specs=[
            pl.BlockSpec((tm, D), lambda i: (i, 0)),
            pl.BlockSpec((1, D), lambda i: (0, 0)),
            pl.BlockSpec((E, D), lambda i: (0, 0)),
        ],
        out_specs=[pl.BlockSpec((tm, D), lambda i: (i, 0)),
                   pl.BlockSpec((per, E, MOE_TILE), lambda i: (i, 0, 0))],
        out_shape=[jax.ShapeDtypeStruct((T, D), BF16),
                   jax.ShapeDtypeStruct((T // MOE_TILE, E, MOE_TILE), F32)],
        compiler_params=_params(1),
        name="router",
    )(x, g, w_rt)


def _strict_upper(n):
    r = lax.broadcasted_iota(jnp.int32, (n, n), 0)
    c = lax.broadcasted_iota(jnp.int32, (n, n), 1)
    return (r < c).astype(BF16)


def _select_kernel(aff_ref, sel_ref, cnt_ref, pos_ref, *, cap):
    NT, E, tm = aff_ref.shape
    capf = float(cap)

    def count_ge(v):
        bits = lax.bitcast_convert_type(aff_ref[...], jnp.int32)
        ge = (bits >= v[None]).astype(F32)
        return jnp.sum(jnp.sum(ge, axis=0), axis=1, keepdims=True)

    def bisect(_, lh):
        lo, hi = lh
        mid = lo + lax.shift_right_logical(hi - lo, 1)
        ok = count_ge(mid) >= capf
        return jnp.where(ok, mid, lo), jnp.where(ok, hi, mid)

    one_bits = 0x3F800000
    lo0 = jnp.zeros((E, 1), jnp.int32)
    hi0 = jnp.full((E, 1), one_bits + 1, jnp.int32)
    tau, _ = lax.fori_loop(0, 31, bisect, (lo0, hi0))
    need = capf - count_ge(tau + 1)
    tri = _strict_upper(tm)

    def tile(j, carry):
        ceq, csel = carry
        bits = lax.bitcast_convert_type(aff_ref[j], jnp.int32)
        gt = bits > tau
        eq = (bits == tau).astype(F32)
        rank_eq = jnp.dot(eq.astype(BF16), tri, preferred_element_type=F32) + ceq
        sel = jnp.where(gt | ((eq > 0) & (rank_eq < need)), 1.0, 0.0)
        sel_ref[j] = sel
        c = jnp.sum(sel, axis=1, keepdims=True)
        cnt_ref[j] = jnp.broadcast_to(c, (E, LANES))
        pos_ref[j] = jnp.broadcast_to(csel, (E, LANES))
        return ceq + jnp.sum(eq, axis=1, keepdims=True), csel + c

    zero = jnp.zeros((E, 1), F32)
    lax.fori_loop(0, NT, tile, (zero, zero))


def _select(aff, cap):
    NT, E, tm = aff.shape
    sel, cnt, pos = pl.pallas_call(
        functools.partial(_select_kernel, cap=cap),
        out_shape=[jax.ShapeDtypeStruct((NT, E, tm), F32),
                   jax.ShapeDtypeStruct((NT, E, LANES), F32),
                   jax.ShapeDtypeStruct((NT, E, LANES), F32)],
        compiler_params=pltpu.CompilerParams(vmem_limit_bytes=VMEM_LIMIT),
        name="expert_select",
    )(aff)
    to_smem = lambda a: a[:, :, 0].astype(jnp.int32).reshape(NT * E)
    return sel, to_smem(cnt), to_smem(pos)


def _stack_rows(E, tm, sub):
    worst = E * tm + E * (sub - 1 + PIECE - 1)
    return -(-worst // CHUNK) * CHUNK


def _cdiv_pow2(x, log2):
    return lax.shift_right_logical(x + ((1 << log2) - 1), log2)


def _segments(cnt_ref, pos_ref, tile, E, sub):
    segs = []
    base = jnp.int32(0)
    for e in range(E):
        n = cnt_ref[tile * E + e]
        p0 = pos_ref[tile * E + e]
        a = p0 & (sub - 1)
        pieces = jnp.where(n > 0, _cdiv_pow2(a + n, LOG2_PIECE), 0)
        segs.append((base, pieces, p0 - a, a, n))
        base = base + pieces * PIECE
    return segs, base


NO_RANK = -256.0
NO_OWNER = 1024.0
OWNER_COPIES = 3


def _segment_rows(segs):
    E = len(segs)
    lane = lax.broadcasted_iota(jnp.int32, (1, LANES), 1)
    used = lane < OWNER_COPIES * E
    first = jnp.full((1, LANES), float(NO_OWNER * NO_OWNER), F32)
    end = jnp.full((1, LANES), -1.0, F32)
    zero = jnp.zeros((1, LANES), F32)
    for e, (base, pieces, _, a, _) in enumerate(segs):
        m = used & ((lane & (E - 1)) == e)
        first = jnp.where(m, base.astype(F32), first)
        end = jnp.where(m, (base + pieces * PIECE).astype(F32), end)
        zero = jnp.where(m, (base + a).astype(F32), zero)
    return first, end, zero


def _tile_ranks(sel, tm):
    E = sel.shape[0]
    rank = jnp.dot(sel.astype(BF16), _strict_upper(tm), preferred_element_type=F32)
    rank = jnp.where(sel > 0, rank, NO_RANK).astype(BF16)
    return jnp.concatenate([rank, jnp.zeros((LANES - E, tm), BF16)], axis=0)


def _chunk_one_hot(j0, rows, ranks, E):
    first, end, zero = rows
    jv = (j0 + lax.broadcasted_iota(jnp.int32, (CHUNK, LANES), 0)).astype(F32)
    own = (jv >= first) & (jv < end)
    lane = lax.broadcasted_iota(jnp.int32, (CHUNK, LANES), 1)
    tgt = jnp.sum(jnp.where(own & (lane < E), jv - zero + NO_OWNER, 0.0), axis=1, keepdims=True) - NO_OWNER
    ownb = jnp.where(own, 1.0, 0.0).astype(BF16)
    hit = jnp.dot(ownb, ranks, preferred_element_type=F32) == tgt
    return ownb, hit


def _total_pieces(segs):
    t = segs[0][1]
    for s in segs[1:]:
        t = t + s[1]
    return t


WAIT_GROUP = 8
LOG2_WAIT_GROUP = 3
assert 1 << LOG2_WAIT_GROUP == WAIT_GROUP


def _wait_pieces(n, copy_of_rows):
    def waits(count, rows):
        def w(_, c):
            copy_of_rows(rows).wait()
            return c
        lax.fori_loop(0, count, w, 0)
    waits(lax.shift_right_logical(n, LOG2_WAIT_GROUP), WAIT_GROUP * PIECE)
    waits(n & (WAIT_GROUP - 1), PIECE)


def _dispatch_kernel(cnt_ref, pos_ref, sel_ref, hn_ref, xe_ref, stack, pend, sem):
    i = pl.program_id(0)
    nt = pl.num_programs(0)
    slot = i % 2
    E, tm = sel_ref.shape

    @pl.when(i == 0)
    def _():
        pend[...] = jnp.zeros_like(pend)

    segs, total = _segments(cnt_ref, pos_ref, i, E, SUB)
    rows = _segment_rows(segs)
    ranks = _tile_ranks(sel_ref[...], tm)

    def chunk(k, c):
        j0 = pl.multiple_of(k * CHUNK, CHUNK)
        _, hit = _chunk_one_hot(j0, rows, ranks, E)
        oh = jnp.where(hit, 1.0, 0.0).astype(BF16)
        stack[slot, pl.ds(j0, CHUNK), :] = jnp.dot(oh, hn_ref[...], preferred_element_type=F32)
        return c

    lax.fori_loop(0, _cdiv_pow2(total, LOG2_CHUNK), chunk, 0)

    for e in range(E):
        base, _, _, a, n = segs[e]

        @pl.when(n > 0)
        def _():
            first = pl.multiple_of(base, PIECE)
            stack[slot, pl.ds(first, SUB), :] = stack[slot, pl.ds(first, SUB), :] + pend[e]
            last = pl.multiple_of(base + lax.shift_right_logical(a + n - 1, LOG2_SUB) * SUB, SUB)
            incomplete = ((a + n) & (SUB - 1)) != 0
            pend[e] = jnp.where(incomplete, stack[slot, pl.ds(last, SUB), :], 0.0)

    def piece_copy(s, src_row, e, dst_row, rows=PIECE):
        return pltpu.make_async_copy(stack.at[s, pl.ds(src_row, rows), :],
                                     xe_ref.at[e, pl.ds(dst_row, rows), :], sem.at[s])

    @pl.when(i > 0)
    def _():
        prev, _ = _segments(cnt_ref, pos_ref, i - 1, E, SUB)
        _wait_pieces(_total_pieces(prev), lambda rows: piece_copy(1 - slot, 0, 0, 0, rows))

    for e in range(E):
        base, pieces, start, _, _ = segs[e]

        def issue(c, carry, base=base, start=start, e=e):
            piece_copy(slot, pl.multiple_of(base + c * PIECE, PIECE), e,
                       pl.multiple_of(start + c * PIECE, SUB)).start()
            return carry

        lax.fori_loop(0, pieces, issue, 0)

    @pl.when(i == nt - 1)
    def _():
        _wait_pieces(_total_pieces(segs), lambda rows: piece_copy(slot, 0, 0, 0, rows))
        cap = xe_ref.shape[1] - PIECE
        stack[slot, pl.ds(0, PIECE), :] = jnp.zeros((PIECE, stack.shape[2]), F32)
        for e in range(E):
            piece_copy(slot, 0, e, cap).start()
        _wait_pieces(E, lambda rows: piece_copy(slot, 0, 0, 0, rows))


def _dispatch(sel, cnt, pos, hn, cap):
    NT, E, tm = sel.shape
    T, D = hn.shape
    return pl.pallas_call(
        _dispatch_kernel,
        grid_spec=pltpu.PrefetchScalarGridSpec(
            num_scalar_prefetch=2,
            grid=(NT,),
            in_specs=[pl.BlockSpec((None, E, tm), lambda i, c, p: (i, 0, 0)),
                      pl.BlockSpec((tm, D), lambda i, c, p: (i, 0))],
            out_specs=pl.BlockSpec(memory_space=pl.ANY),
            scratch_shapes=[pltpu.VMEM((2, _stack_rows(E, tm, SUB), D), F32),
                            pltpu.VMEM((E, SUB, D), F32),
                            pltpu.SemaphoreType.DMA((2,))]),
        out_shape=jax.ShapeDtypeStruct((E, cap + PIECE, D), F32),
        compiler_params=_params(1),
        name="expert_dispatch",
    )(cnt, pos, sel, hn)


def _segment_cols(segs):
    E = len(segs)
    eidx = lax.broadcasted_iota(jnp.int32, (E, 1), 0)
    first = jnp.zeros((E, 1), F32)
    end = jnp.zeros((E, 1), F32)
    zero = jnp.zeros((E, 1), F32)
    for e, (base, pieces, _, a, _) in enumerate(segs):
        m = eidx == e
        first = jnp.where(m, base.astype(F32), first)
        end = jnp.where(m, (base + pieces * PIECE).astype(F32), end)
        zero = jnp.where(m, (base + a).astype(F32), zero)
    return first, end, zero


def _combine_kernel(cnt_ref, pos_ref, sel_ref, aff_ref, x_ref, y_hbm, o_ref, stack, sem):
    i = pl.program_id(0)
    nt = pl.num_programs(0)
    slot = i % 2
    E, tm = sel_ref.shape

    def piece_copy(s, e, src_row, dst_row, rows=PIECE):
        return pltpu.make_async_copy(y_hbm.at[e, pl.ds(src_row, rows), :],
                                     stack.at[s, pl.ds(dst_row, rows), :], sem.at[s])

    def fetch(tile, s):
        segs, _ = _segments(cnt_ref, pos_ref, tile, E, SUB_BF16)
        for e in range(E):
            base, pieces, start, _, _ = segs[e]

            def issue(c, carry, base=base, start=start, e=e):
                piece_copy(s, e, pl.multiple_of(start + c * PIECE, SUB_BF16),
                           pl.multiple_of(base + c * PIECE, PIECE)).start()
                return carry

            lax.fori_loop(0, pieces, issue, 0)

    @pl.when(i == 0)
    def _():
        stack[...] = jnp.zeros_like(stack)
        fetch(0, 0)

    @pl.when(i + 1 < nt)
    def _():
        fetch(i + 1, 1 - slot)

    segs, total = _segments(cnt_ref, pos_ref, i, E, SUB_BF16)
    first, end, zero = _segment_cols(segs)
    sel = sel_ref[...]
    aff = aff_ref[...]
    a_hi = aff.astype(BF16).astype(F32)
    a_mid = (aff - a_hi).astype(BF16).astype(F32)
    a_lo = ((aff - a_hi) - a_mid).astype(BF16).astype(F32)
    rank = jnp.dot(sel.astype(BF16), _strict_upper(tm), preferred_element_type=F32)
    rank = jnp.where(sel > 0, rank, NO_RANK)
    pad = jnp.zeros((LANES - (OWNER_COPIES + 1) * E, tm), F32)
    by_token = jnp.transpose(jnp.concatenate([a_hi, a_mid, a_lo, rank, pad], axis=0)).astype(BF16)
    o_ref[...] = x_ref[...]
    _wait_pieces(_total_pieces(segs), lambda rows: piece_copy(slot, 0, 0, 0, rows))

    def chunk(k, c):
        j0 = pl.multiple_of(k * CHUNK, CHUNK)
        jl =(j0 + lax.broadcasted_iota(jnp.int32, (E, CHUNK), 1)).astype(F32)
        own = (jl >= first) & (jl < end)
        tgt = jnp.sum(jnp.where(own, jl - zero + NO_OWNER, 0.0), axis=0, keepdims=True) - NO_OWNER
        o16 = jnp.where(own, 1.0, 0.0).astype(BF16)
        zeros = lambda n: jnp.zeros((n, CHUNK), BF16)
        own_aff = jnp.concatenate([o16] * OWNER_COPIES + [zeros(LANES - OWNER_COPIES * E)], axis=0)
        own_rank = jnp.concatenate([zeros(OWNER_COPIES * E), o16, zeros(LANES - (OWNER_COPIES + 1) * E)], axis=0)
        hit = jnp.dot(by_token, own_rank, preferred_element_type=F32) == tgt
        g = jnp.where(hit, jnp.dot(by_token, own_aff, preferred_element_type=F32), 0.0)
        g_hi = g.astype(BF16)
        g_lo = (g - g_hi.astype(F32)).astype(BF16)
        y = stack[slot, pl.ds(j0, CHUNK), :]
        o_ref[...] += (jnp.dot(g_hi, y, preferred_element_type=F32)
                       + jnp.dot(g_lo, y, preferred_element_type=F32))
        return c

    lax.fori_loop(0, _cdiv_pow2(total, LOG2_CHUNK), chunk, 0)


def _combine(sel, aff, cnt, pos, x, y):
    NT, E, tm = sel.shape
    T, D = x.shape
    tile_spec = pl.BlockSpec((None, E, tm), lambda i, c, p: (i, 0, 0))
    x_spec = pl.BlockSpec((tm, D), lambda i, c, p: (i, 0))
    return pl.pallas_call(
        _combine_kernel,
        grid_spec=pltpu.PrefetchScalarGridSpec(
            num_scalar_prefetch=2,
            grid=(NT,),
            in_specs=[tile_spec, tile_spec, x_spec, pl.BlockSpec(memory_space=pl.ANY)],
            out_specs=x_spec,
            scratch_shapes=[pltpu.VMEM((2, _stack_rows(E, tm, SUB_BF16), D), BF16),
                            pltpu.SemaphoreType.DMA((2,))]),
        out_shape=jax.ShapeDtypeStruct((T, D), F32),
        compiler_params=_params(1),
        name="expert_combine",
    )(cnt, pos, sel, aff, x, y)


def _ffn_kernel(x_ref, wg_ref, wu_ref, wd_ref, y_ref):
    i = pl.program_id(1)
    last = pl.num_programs(1) - 1

    @pl.when(i < last)
    def _():
        x = x_ref[...].astype(BF16)
        a = jnp.dot(x, wg_ref[...], preferred_element_type=F32)
        b = jnp.dot(x, wu_ref[...], preferred_element_type=F32)
        h = (a * (1.0 / (1.0 + jnp.exp(-a)))) * b
        y_ref[...] = jnp.dot(h.astype(BF16), wd_ref[...], preferred_element_type=F32).astype(y_ref.dtype)

    @pl.when(i == last)
    def _():
        y_ref[...] = jnp.zeros_like(y_ref)


def _expert_ffn(xe, cap, wg, wu, wd, layer):
    E, _, D = xe.shape
    F = wg.shape[3]
    tm = min(ROW_TILE, cap)
    n = cap // tm
    return pl.pallas_call(
        _ffn_kernel,
        grid=(E, n + 1),
        in_specs=[
            pl.BlockSpec((None, tm, D), lambda e, i: (e, jnp.minimum(i, n - 1), 0)),
            pl.BlockSpec((None, None, D, F), lambda e, i: (layer, e, 0, 0)),
            pl.BlockSpec((None, None, D, F), lambda e, i: (layer, e, 0, 0)),
            pl.BlockSpec((None, None, F, D), lambda e, i: (layer, e, 0, 0)),
        ],
        out_specs=pl.BlockSpec((None, tm, D), lambda e, i: (e, i, 0)),
        out_shape=jax.ShapeDtypeStruct((E, cap + tm, D), BF16),
        compiler_params=_params(2),
        name=f"expert_ffn_cap{cap}",
    )(xe, wg, wu, wd)


def _norm_kernel(x_ref, g_ref, y_ref):
    y_ref[...] = _rms(x_ref[...], g_ref[...])


def _final_norm(x, g):
    T, D = x.shape
    tm = ROW_TILE
    return pl.pallas_call(
        _norm_kernel,
        grid=(T // tm,),
        in_specs=[pl.BlockSpec((tm, D), lambda i: (i, 0)), pl.BlockSpec((1, D), lambda i: (0, 0))],
        out_specs=pl.BlockSpec((tm, D), lambda i: (i, 0)),
        out_shape=jax.ShapeDtypeStruct((T, D), F32),
        compiler_params=_params(1),
        name="final_norm",
    )(x, g)


def _moe(x, g, w_rt, wg, wu, wd, layer):
    T, D = x.shape
    hn, aff = _router(x, g, w_rt)
    cap = max(1, (EC_CAPACITY * T) // N_EXPERTS)
    sel, cnt, pos = _select(aff, cap)
    xe = _dispatch(sel, cnt, pos, hn, cap)
    y = _expert_ffn(xe, cap, wg, wu, wd, layer)
    return _combine(sel, aff, cnt, pos, x, y)


def _trunk(x3, p):
    B, S, D = x3.shape
    x = x3.reshape(B * S, D)
    tables = _rope_tables(S)
    depth = p["ffn_norm"].shape[0]
    for i in range(depth):
        j = i // 2
        if i % 2 == 0:
            qkvs = _qkv_proj(x, p["attn_norm"][j][None], p["w_qkv"][j], tables, B, S)
            res = [_dilated_attention(qkvs[g], g) for g in range(N_GROUPS)]
            x = _attn_out_proj([r[0] for r in res], [r[1] for r in res], p["w_attn_out"][j], x, S)
        else:
            x = _short_conv(x, p["conv_norm"][j][None], p["w_conv_in"][j], p["conv_w"][j],
                            p["w_conv_out"][j], S)
        x = _moe(x, p["ffn_norm"][i][None], p["w_router_t"][i], p["w_gate"], p["w_up"], p["w_down"], i)
    return _final_norm(x, p["final_norm"][None]).reshape(B, S, D)


def kernel(x_prompt, x_sample, attn_norm, w_qkv, w_attn_out, conv_norm, w_conv_in, conv_w, w_conv_out,
           ffn_norm, w_router, w_gate, w_up, w_down, final_norm):
    per_layer = lambda w: [w[i].astype(BF16) for i in range(w.shape[0])]
    p = dict(
        attn_norm=attn_norm, conv_norm=conv_norm, ffn_norm=ffn_norm, final_norm=final_norm, conv_w=conv_w,
        w_qkv=per_layer(w_qkv), w_attn_out=per_layer(w_attn_out),
        w_conv_in=per_layer(w_conv_in), w_conv_out=per_layer(w_conv_out),
        w_router_t=per_layer(jnp.swapaxes(w_router, 1, 2)),
        w_gate=w_gate.astype(BF16), w_up=w_up.astype(BF16), w_down=w_down.astype(BF16),
    )
    return (_trunk(x_prompt, p), _trunk(x_sample, p))
```

```python
import functools

import jax
import jax.numpy as jnp
from jax import lax
from jax.experimental import pallas as pl
from jax.experimental.pallas import tpu as pltpu

D_MODEL = 1024
HEAD_DIM = 64
HEADS_PER_GROUP = 4
DILATED_GROUPS = ((128, 1), (512, 4), (2048, 16))
N_GROUPS = len(DILATED_GROUPS)
ATTN_WIDTH = N_GROUPS * HEADS_PER_GROUP * HEAD_DIM
GROUP_WIDTH = HEADS_PER_GROUP * HEAD_DIM
ROPE_DIM = HEAD_DIM // 4
ROPE_THETA = 500000.0
N_EXPERTS = 16
EC_CAPACITY = 2
RMS_EPS = 1e-6
NEG_INF = -1e30

LANES = 128
SUB = 8
SUB_BF16 = 16
ROW_TILE = 512
ATTN_TQ = 128
MOE_TILE = 256
CHUNK = 256
PIECE = 16
LOG2_PIECE = 4
LOG2_CHUNK = 8
LOG2_SUB = 3
assert (1 << LOG2_PIECE, 1 << LOG2_CHUNK, 1 << LOG2_SUB) == (PIECE, CHUNK, SUB)
VMEM_LIMIT = 56 * 1024 * 1024

F32 = jnp.float32
BF16 = jnp.bfloat16


def _params(n_axes):
    return pltpu.CompilerParams(
        dimension_semantics=("arbitrary",) * n_axes, vmem_limit_bytes=VMEM_LIMIT)


def _rms(x, g):
    ms = jnp.mean(x * x, axis=-1, keepdims=True)
    return x * lax.rsqrt(ms + RMS_EPS) * g


def _qkv_kernel(x_ref, g_ref, w_ref, cos_ref, sa_ref, sb_ref, o0_ref, o1_ref, o2_ref, s_ref):
    tm = x_ref.shape[0]
    h = _rms(x_ref[...], g_ref[...]).astype(BF16)
    acc = jnp.dot(h, w_ref[...], preferred_element_type=F32)
    c, sa, sb = cos_ref[...], sa_ref[...], sb_ref[...]
    n_q = ATTN_WIDTH // LANES
    for j in range(2 * n_q):
        blk = acc[:, j * LANES:(j + 1) * LANES]
        r = blk * c + pltpu.roll(blk, LANES - ROPE_DIM // 2, 1) * sa + pltpu.roll(blk, ROPE_DIM // 2, 1) * sb
        if j < n_q:
            r = r * (HEAD_DIM ** -0.5)
        s_ref[j] = r
    for j in range(2 * n_q, 3 * n_q):
        s_ref[j] = acc[:, j * LANES:(j + 1) * LANES]
    per_group = GROUP_WIDTH // LANES
    for g, o_ref in enumerate((o0_ref, o1_ref, o2_ref)):
        dil = DILATED_GROUPS[g][1]
        for part in range(3):
            for jj in range(per_group):
                src = part * n_q + g * per_group + jj
                dst = slice((part * per_group + jj) * LANES, (part * per_group + jj + 1) * LANES)
                for r in range(dil):
                    rows = pl.ds(r, tm // dil, stride=dil) if dil > 1 else slice(None)
                    o_ref[r, :, dst] = s_ref[src, rows, :].astype(BF16)


def _rope_tables(S):
    half = ROPE_DIM // 2
    inv_freq = jnp.power(ROPE_THETA, -jnp.arange(0, ROPE_DIM, 2, dtype=F32) / ROPE_DIM)
    ang = jnp.arange(S, dtype=F32)[:, None] * inv_freq[None, :]
    cos, sin = jnp.cos(ang), jnp.sin(ang)
    rest = HEAD_DIM - ROPE_DIM
    c = jnp.concatenate([cos, cos, jnp.ones((S, rest), F32)], axis=1)
    sa = jnp.concatenate([-sin, jnp.zeros((S, half + rest), F32)], axis=1)
    sb = jnp.concatenate([jnp.zeros((S, half), F32), sin, jnp.zeros((S, rest), F32)], axis=1)
    rep = LANES // HEAD_DIM
    return tuple(jnp.tile(t, (1, rep)) for t in (c, sa, sb))


def _qkv_proj(x, g, w, tables, B, S):
    T, D = x.shape
    N = w.shape[1]
    tm = ROW_TILE
    per_seq = S // tm
    tab_spec = pl.BlockSpec((tm, LANES), lambda i: (i % per_seq, 0))
    dils = [d for _, d in DILATED_GROUPS]
    return pl.pallas_call(
        _qkv_kernel,
        grid=(T // tm,),
        in_specs=[
            pl.BlockSpec((tm, D), lambda i: (i, 0)),
            pl.BlockSpec((1, D), lambda i: (0, 0)),
            pl.BlockSpec((D, N), lambda i: (0, 0)),
            tab_spec, tab_spec, tab_spec,
        ],
        out_specs=[pl.BlockSpec((None, d, tm // d, 3 * GROUP_WIDTH), lambda i: (i // per_seq, 0, i % per_seq, 0))
                   for d in dils],
        out_shape=[jax.ShapeDtypeStruct((B, d, S // d, 3 * GROUP_WIDTH), BF16) for d in dils],
        scratch_shapes=[pltpu.VMEM((N // LANES, tm, LANES), F32)],
        compiler_params=_params(1),
        name="qkv_proj",
    )(x, g, w, *tables)


def _attn_kernel(q_ref, k_ref, v_ref, o_ref, l_ref, *, L, Lq, kw, tq, radius):
    c = pl.program_id(2)

    def body(i, carry):
        r0 = pl.multiple_of(i * tq, tq)
        t0 = c * Lq + i * tq
        ks = pl.multiple_of(jnp.clip(t0 - radius, 0, L - kw), radius)
        qb = q_ref[pl.ds(r0, tq), :]
        kb = k_ref[pl.ds(ks, kw), :]
        vb = v_ref[pl.ds(ks, kw), :]
        qi = t0 + lax.broadcasted_iota(jnp.int32, (tq, kw), 0)
        kj = ks + lax.broadcasted_iota(jnp.int32, (tq, kw), 1)
        valid = jnp.abs(qi - kj) <= radius
        outs, lses = [], []
        for h in range(HEADS_PER_GROUP):
            sl = slice(h * HEAD_DIM, (h + 1) * HEAD_DIM)
            s = lax.dot_general(qb[:, sl], kb[:, sl], (((1,), (1,)), ((), ())),
                                preferred_element_type=F32)
            s = jnp.where(valid, s, NEG_INF)
            m = jnp.max(s, axis=-1, keepdims=True)
            p = jnp.exp(s - m)
            l = jnp.sum(p, axis=-1, keepdims=True)
            o = jnp.dot(p.astype(BF16), vb[:, sl], preferred_element_type=F32) / l
            outs.append(o)
            lses.append(jnp.broadcast_to(m + jnp.log(l), (tq, HEAD_DIM)))
        o_ref[pl.ds(r0, tq), :] = jnp.concatenate(outs, axis=1)
        l_ref[pl.ds(r0, tq), :] = jnp.concatenate(lses, axis=1)
        return carry

    n_blk = Lq // tq
    lax.fori_loop(0, n_blk, body, 0, unroll=2 if n_blk % 2 == 0 else 1)


def _dilated_attention(qkv, group):
    window, dil = DILATED_GROUPS[group]
    B, _, L, _ = qkv.shape
    radius = window // (2 * dil)
    tq = min(ATTN_TQ, L)
    kw = min(L, tq + 2 * radius)
    Lq = min(L, 1024)
    assert L % tq == 0 and L % Lq == 0 and Lq % tq == 0 and radius % 16 == 0 and (L - kw) % radius == 0
    kern = functools.partial(_attn_kernel, L=L, Lq=Lq, kw=kw, tq=tq, radius=radius)
    out_spec = pl.BlockSpec((None, None, Lq, GROUP_WIDTH), lambda b, r, c: (b, r, c, 0))
    return pl.pallas_call(
        kern,
        grid=(B, dil, L // Lq),
        in_specs=[
            pl.BlockSpec((None, None, Lq, GROUP_WIDTH), lambda b, r, c: (b, r, c, 0)),
            pl.BlockSpec((None, None, L, GROUP_WIDTH), lambda b, r, c: (b, r, 0, 1)),
            pl.BlockSpec((None, None, L, GROUP_WIDTH), lambda b, r, c: (b, r, 0, 2)),
        ],
        out_specs=[out_spec, out_spec],
        out_shape=[jax.ShapeDtypeStruct((B, dil, L, GROUP_WIDTH), F32)] * 2,
        compiler_params=_params(3),
        name=f"dilated_attn_g{group}",
    )(qkv, qkv, qkv)


def _oproj_kernel(o0_ref, o1_ref, o2_ref, l0_ref, l1_ref, l2_ref, w_ref, x_ref, gf_ref, wr_ref,
                  y_ref, hn_ref, aff_ref, o_scr, l_scr):
    tm = x_ref.shape[0]
    for g, (o_ref, l_ref) in enumerate(((o0_ref, l0_ref), (o1_ref, l1_ref), (o2_ref, l2_ref))):
        dil = DILATED_GROUPS[g][1]
        for r in range(dil):
            rows = pl.ds(r, tm // dil, stride=dil) if dil > 1 else slice(None)
            for jj in range(GROUP_WIDTH // LANES):
                lanes = slice(jj * LANES, (jj + 1) * LANES)
                o_scr[g, jj, rows, :] = o_ref[r, :, lanes]
                l_scr[g, jj, rows, :] = l_ref[r, :, lanes]
    zs = []
    for jj in range(GROUP_WIDTH // LANES):
        l0, l1, l2 = l_scr[0, jj], l_scr[1, jj], l_scr[2, jj]
        m = jnp.maximum(jnp.maximum(l0, l1), l2)
        e0, e1, e2 = jnp.exp(l0 - m), jnp.exp(l1 - m), jnp.exp(l2 - m)
        den = e0 + e1 + e2
        zs.append((o_scr[0, jj] * (e0 / den), o_scr[1, jj] * (e1 / den), o_scr[2, jj] * (e2 / den)))
    z = jnp.concatenate([zs[jj][g] for g in range(N_GROUPS) for jj in range(GROUP_WIDTH // LANES)], axis=1)
    y = x_ref[...] + jnp.dot(z.astype(BF16), w_ref[...], preferred_element_type=F32)
    y_ref[...] = y
    _route(y, gf_ref, wr_ref, hn_ref, aff_ref)


def _attn_out_proj(outs, lses, w, x, g_ffn, w_rt, S):
    T, D = x.shape
    tm = ROW_TILE
    per_seq = S // tm
    gspecs = [pl.BlockSpec((None, d, tm // d, GROUP_WIDTH), lambda i: (i // per_seq, 0, i % per_seq, 0))
              for _, d in DILATED_GROUPS]
    xspec = pl.BlockSpec((tm, D), lambda i: (i, 0))
    r_in, r_out, r_shapes = _route_specs(T, D, w_rt.shape[0], tm)
    return pl.pallas_call(
        _oproj_kernel,
        grid=(T // tm,),
        in_specs=gspecs + gspecs + [pl.BlockSpec((ATTN_WIDTH, D), lambda i: (0, 0)), xspec] + r_in,
        out_specs=[xspec] + r_out,
        out_shape=[jax.ShapeDtypeStruct((T, D), F32)] + r_shapes,
        scratch_shapes=[pltpu.VMEM((N_GROUPS, GROUP_WIDTH // LANES, tm, LANES), F32)] * 2,
        compiler_params=_params(1),
        name="attn_out_proj",
    )(*outs, *lses, w, x, g_ffn, w_rt)


def _route(x, gf_ref, wr_ref, hn_ref, aff_ref):
    h = _rms(x, gf_ref[...]).astype(BF16)
    hn_ref[...] = h
    logits = lax.dot_general(wr_ref[...], h, (((1,), (1,)), ((), ())),
                             preferred_element_type=F32)
    m = jnp.max(logits, axis=0, keepdims=True)
    e = jnp.exp(logits - m)
    aff = e / jnp.sum(e, axis=0, keepdims=True)
    for k in range(aff_ref.shape[0]):
        aff_ref[k] = aff[:, k * MOE_TILE:(k + 1) * MOE_TILE]


def _route_specs(T, D, E, tm):
    per = tm // MOE_TILE
    in_specs = [pl.BlockSpec((1, D), lambda i: (0, 0)), pl.BlockSpec((E, D), lambda i: (0, 0))]
    out_specs = [pl.BlockSpec((tm, D), lambda i: (i, 0)), pl.BlockSpec((per, E, MOE_TILE), lambda i: (i, 0, 0))]
    out_shapes = [jax.ShapeDtypeStruct((T, D), BF16), jax.ShapeDtypeStruct((T // MOE_TILE, E, MOE_TILE), F32)]
    return in_specs, out_specs, out_shapes


def _conv_kernel(x_ref, xp_ref, xn_ref, g_ref, win_ref, cw_ref, wout_ref, gf_ref, wr_ref,
                 y_ref, hn_ref, aff_ref, u_scr, *, tm, per_seq):
    i = pl.program_id(0)
    D = x_ref.shape[1]
    x = x_ref[...]
    xa = jnp.concatenate([xp_ref[...], x, xn_ref[...]], axis=0)
    h = _rms(xa, g_ref[...]).astype(BF16)
    full = jnp.dot(h, win_ref[...], preferred_element_type=F32)
    u_scr[...] = full[:, D:2 * D] * full[:, 2 * D:]
    first = (i % per_seq) == 0
    last = (i % per_seq) == per_seq - 1
    u_scr[pl.ds(7, 1), :] = jnp.where(first, 0.0, u_scr[pl.ds(7, 1), :])
    u_scr[pl.ds(tm + 8, 1), :] = jnp.where(last, 0.0, u_scr[pl.ds(tm + 8, 1), :])
    cw = cw_ref[...]
    conv = (cw[0:1] * u_scr[pl.ds(7, tm), :] + cw[1:2] * u_scr[pl.ds(8, tm), :]
            + cw[2:3] * u_scr[pl.ds(9, tm), :])
    z = (full[8:8 + tm, :D] * conv).astype(BF16)
    y = x + jnp.dot(z, wout_ref[...], preferred_element_type=F32)
    y_ref[...] = y
    _route(y, gf_ref, wr_ref, hn_ref, aff_ref)


def _short_conv(x, g, w_in, cw, w_out, g_ffn, w_rt, S):
    T, D = x.shape
    tm = ROW_TILE
    per_seq = S // tm
    sub = tm // 8
    n8 = T // 8
    kern = functools.partial(_conv_kernel, tm=tm, per_seq=per_seq)
    r_in, r_out, r_shapes = _route_specs(T, D, w_rt.shape[0], tm)
    return pl.pallas_call(
        kern,
        grid=(T // tm,),
        in_specs=[
            pl.BlockSpec((tm, D), lambda i: (i, 0)),
            pl.BlockSpec((8, D), lambda i: (jnp.maximum(i * sub - 1, 0), 0)),
            pl.BlockSpec((8, D), lambda i: (jnp.minimum((i + 1) * sub, n8 - 1), 0)),
            pl.BlockSpec((1, D), lambda i: (0, 0)),
            pl.BlockSpec((D, 3 * D), lambda i: (0, 0)),
            pl.BlockSpec((3, D), lambda i: (0, 0)),
            pl.BlockSpec((D, D), lambda i: (0, 0)),
        ] + r_in,
        out_specs=[pl.BlockSpec((tm, D), lambda i: (i, 0))] + r_out,
        out_shape=[jax.ShapeDtypeStruct((T, D), F32)] + r_shapes,
        scratch_shapes=[pltpu.VMEM((tm + 16, D), F32)],
        compiler_params=_params(1),
        name="short_conv",
    )(x, x, x, g, w_in, cw, w_out, g_ffn, w_rt)


def _strict_upper(n):
    r = lax.broadcasted_iota(jnp.int32, (n, n), 0)
    c = lax.broadcasted_iota(jnp.int32, (n, n), 1)
    return (r < c).astype(BF16)


def _select_kernel(aff_ref, sel_ref, cnt_ref, pos_ref, *, cap):
    NT, E, tm = aff_ref.shape
    capf = float(cap)

    def count_ge(v):
        bits = lax.bitcast_convert_type(aff_ref[...], jnp.int32)
        ge = (bits >= v[None]).astype(F32)
        return jnp.sum(jnp.sum(ge, axis=0), axis=1, keepdims=True)

    def bisect(_, lh):
        lo, hi = lh
        mid = lo + lax.shift_right_logical(hi - lo, 1)
        ok = count_ge(mid) >= capf
        return jnp.where(ok, mid, lo), jnp.where(ok, hi, mid)

    one_bits = 0x3F800000
    lo0 = jnp.zeros((E, 1), jnp.int32)
    hi0 = jnp.full((E, 1), one_bits + 1, jnp.int32)
    tau, _ = lax.fori_loop(0, 31, bisect, (lo0, hi0))
    need = capf - count_ge(tau + 1)
    tri = _strict_upper(tm)

    def tile(j, carry):
        ceq, csel = carry
        bits = lax.bitcast_convert_type(aff_ref[j], jnp.int32)
        gt = bits > tau
        eq = (bits == tau).astype(F32)
        rank_eq = jnp.dot(eq.astype(BF16), tri, preferred_element_type=F32) + ceq
        sel = jnp.where(gt | ((eq > 0) & (rank_eq < need)), 1.0, 0.0)
        sel_ref[j] = sel
        c = jnp.sum(sel, axis=1, keepdims=True)
        cnt_ref[j] = jnp.broadcast_to(c, (E, LANES))
        pos_ref[j] = jnp.broadcast_to(csel, (E, LANES))
        return ceq + jnp.sum(eq, axis=1, keepdims=True), csel + c

    zero = jnp.zeros((E, 1), F32)
    lax.fori_loop(0, NT, tile, (zero, zero))


def _select(aff, cap):
    NT, E, tm = aff.shape
    sel, cnt, pos = pl.pallas_call(
        functools.partial(_select_kernel, cap=cap),
        out_shape=[jax.ShapeDtypeStruct((NT, E, tm), F32),
                   jax.ShapeDtypeStruct((NT, E, LANES), F32),
                   jax.ShapeDtypeStruct((NT, E, LANES), F32)],
        compiler_params=pltpu.CompilerParams(vmem_limit_bytes=VMEM_LIMIT),
        name="expert_select",
    )(aff)
    to_smem = lambda a: a[:, :, 0].astype(jnp.int32).reshape(NT * E)
    return sel, to_smem(cnt), to_smem(pos)


def _stack_rows(E, tm, sub):
    worst = E * tm + E * (sub - 1 + PIECE - 1)
    return -(-worst // CHUNK) * CHUNK


def _cdiv_pow2(x, log2):
    return lax.shift_right_logical(x + ((1 << log2) - 1), log2)


def _segments(cnt_ref, pos_ref, tile, E, sub):
    segs = []
    base = jnp.int32(0)
    for e in range(E):
        n = cnt_ref[tile * E + e]
        p0 = pos_ref[tile * E + e]
        a = p0 & (sub - 1)
        pieces = jnp.where(n > 0, _cdiv_pow2(a + n, LOG2_PIECE), 0)
        segs.append((base, pieces, p0 - a, a, n))
        base = base + pieces * PIECE
    return segs, base


NO_RANK = -256.0
NO_OWNER = 1024.0
OWNER_COPIES = 3


def _segment_rows(segs):
    E = len(segs)
    lane = lax.broadcasted_iota(jnp.int32, (1, LANES), 1)
    used = lane < OWNER_COPIES * E
    first = jnp.full((1, LANES), float(NO_OWNER * NO_OWNER), F32)
    end = jnp.full((1, LANES), -1.0, F32)
    zero = jnp.zeros((1, LANES), F32)
    for e, (base, pieces, _, a, _) in enumerate(segs):
        m = used & ((lane & (E - 1)) == e)
        first = jnp.where(m, base.astype(F32), first)
        end = jnp.where(m, (base + pieces * PIECE).astype(F32), end)
        zero = jnp.where(m, (base + a).astype(F32), zero)
    return first, end, zero


def _tile_ranks(sel, tm):
    E = sel.shape[0]
    rank = jnp.dot(sel.astype(BF16), _strict_upper(tm), preferred_element_type=F32)
    rank = jnp.where(sel > 0, rank, NO_RANK).astype(BF16)
    return jnp.concatenate([rank, jnp.zeros((LANES - E, tm), BF16)], axis=0)


def _chunk_one_hot(j0, rows, ranks, E):
    first, end, zero = rows
    jv = (j0 + lax.broadcasted_iota(jnp.int32, (CHUNK, LANES), 0)).astype(F32)
    own = (jv >= first) & (jv < end)
    lane = lax.broadcasted_iota(jnp.int32, (CHUNK, LANES), 1)
    tgt = jnp.sum(jnp.where(own & (lane < E), jv - zero + NO_OWNER, 0.0), axis=1, keepdims=True) - NO_OWNER
    ownb = jnp.where(own, 1.0, 0.0).astype(BF16)
    hit = jnp.dot(ownb, ranks, preferred_element_type=F32) == tgt
    return ownb, hit


def _total_pieces(segs):
    t = segs[0][1]
    for s in segs[1:]:
        t = t + s[1]
    return t


WAIT_GROUP = 8
LOG2_WAIT_GROUP = 3
assert 1 << LOG2_WAIT_GROUP == WAIT_GROUP


def _wait_pieces(n, copy_of_rows):
    def waits(count, rows):
        def w(_, c):
            copy_of_rows(rows).wait()
            return c
        lax.fori_loop(0, count, w, 0)
    waits(lax.shift_right_logical(n, LOG2_WAIT_GROUP), WAIT_GROUP * PIECE)
    waits(n & (WAIT_GROUP - 1), PIECE)


def _dispatch_kernel(cnt_ref, pos_ref, sel_ref, hn_ref, xe_ref, stack, pend, sem):
    i = pl.program_id(0)
    nt = pl.num_programs(0)
    slot = i % 2
    E, tm = sel_ref.shape

    @pl.when(i == 0)
    def _():
        pend[...] = jnp.zeros_like(pend)

    segs, total = _segments(cnt_ref, pos_ref, i, E, SUB)
    rows = _segment_rows(segs)
    ranks = _tile_ranks(sel_ref[...], tm)

    def chunk(k, c):
        j0 = pl.multiple_of(k * CHUNK, CHUNK)
        _, hit = _chunk_one_hot(j0, rows, ranks, E)
        oh = jnp.where(hit, 1.0, 0.0).astype(BF16)
        stack[slot, pl.ds(j0, CHUNK), :] = jnp.dot(oh, hn_ref[...], preferred_element_type=F32)
        return c

    lax.fori_loop(0, _cdiv_pow2(total, LOG2_CHUNK), chunk, 0)

    for e in range(E):
        base, _, _, a, n = segs[e]

        @pl.when(n > 0)
        def _():
            first = pl.multiple_of(base, PIECE)
            stack[slot, pl.ds(first, SUB), :] = stack[slot, pl.ds(first, SUB), :] + pend[e]
            last = pl.multiple_of(base + lax.shift_right_logical(a + n - 1, LOG2_SUB) * SUB, SUB)
            incomplete = ((a + n) & (SUB - 1)) != 0
            pend[e] = jnp.where(incomplete, stack[slot, pl.ds(last, SUB), :], 0.0)

    def piece_copy(s, src_row, e, dst_row, rows=PIECE):
        return pltpu.make_async_copy(stack.at[s, pl.ds(src_row, rows), :],
                                     xe_ref.at[e, pl.ds(dst_row, rows), :], sem.at[s])

    @pl.when(i > 0)
    def _():
        prev, _ = _segments(cnt_ref, pos_ref, i - 1, E, SUB)
        _wait_pieces(_total_pieces(prev), lambda rows: piece_copy(1 - slot, 0, 0, 0, rows))

    for e in range(E):
        base, pieces, start, _, _ = segs[e]

        def issue(c, carry, base=base, start=start, e=e):
            piece_copy(slot, pl.multiple_of(base + c * PIECE, PIECE), e,
                       pl.multiple_of(start + c * PIECE, SUB)).start()
            return carry

        lax.fori_loop(0, pieces, issue, 0)

    @pl.when(i == nt - 1)
    def _():
        _wait_pieces(_total_pieces(segs), lambda rows: piece_copy(slot, 0, 0, 0, rows))
        cap = xe_ref.shape[1] - PIECE
        stack[slot, pl.ds(0, PIECE), :] = jnp.zeros((PIECE, stack.shape[2]), F32)
        for e in range(E):
            piece_copy(slot, 0, e, cap).start()
        _wait_pieces(E, lambda rows: piece_copy(slot, 0, 0, 0, rows))


def _dispatch(sel, cnt, pos, hn, cap):
    NT, E, tm = sel.shape
    T, D = hn.shape
    return pl.pallas_call(
        _dispatch_kernel,
        grid_spec=pltpu.PrefetchScalarGridSpec(
            num_scalar_prefetch=2,
            grid=(NT,),
            in_specs=[pl.BlockSpec((None, E, tm), lambda i, c, p: (i, 0, 0)),
                      pl.BlockSpec((tm, D), lambda i, c, p: (i, 0))],
            out_specs=pl.BlockSpec(memory_space=pl.ANY),
            scratch_shapes=[pltpu.VMEM((2, _stack_rows(E, tm, SUB), D), F32),
                            pltpu.VMEM((E, SUB, D), F32),
                            pltpu.SemaphoreType.DMA((2,))]),
        out_shape=jax.ShapeDtypeStruct((E, cap + PIECE, D), F32),
        compiler_params=_params(1),
        name="expert_dispatch",
    )(cnt, pos, sel, hn)


def _segment_cols(segs):
    E = len(segs)
    eidx = lax.broadcasted_iota(jnp.int32, (E, 1), 0)
    first = jnp.zeros((E, 1), F32)
    end = jnp.zeros((E, 1), F32)
    zero = jnp.zeros((E, 1), F32)
    for e, (base, pieces, _, a, _) in enumerate(segs):
        m = eidx == e
        first = jnp.where(m, base.astype(F32), first)
        end = jnp.where(m, (base + pieces * PIECE).astype(F32), end)
        zero = jnp.where(m, (base + a).astype(F32), zero)
    return first, end, zero


def _combine_kernel(cnt_ref, pos_ref, sel_ref, aff_ref, x_ref, y_hbm, *rest):
    g_ref = rest[0] if len(rest) == 4 else None
    o_ref, stack, sem = rest[-3:]
    i = pl.program_id(0)
    nt = pl.num_programs(0)
    slot = i % 2
    E, tm = sel_ref.shape

    def piece_copy(s, e, src_row, dst_row, rows=PIECE):
        return pltpu.make_async_copy(y_hbm.at[e, pl.ds(src_row, rows), :],
                                     stack.at[s, pl.ds(dst_row, rows), :], sem.at[s])

    def fetch(tile, s):
        segs, _ = _segments(cnt_ref, pos_ref, tile, E, SUB_BF16)
        for e in range(E):
            base, pieces, start, _, _ = segs[e]

            def issue(c, carry, base=base, start=start, e=e):
                piece_copy(s, e, pl.multiple_of(start + c * PIECE, SUB_BF16),
                           pl.multiple_of(base + c * PIECE, PIECE)).start()
                return carry

            lax.fori_loop(0, pieces, issue, 0)

    @pl.when(i == 0)
    def _():
        stack[...] = jnp.zeros_like(stack)
        fetch(0, 0)

    @pl.when(i + 1 < nt)
    def _():
        fetch(i + 1, 1 - slot)

    segs, total = _segments(cnt_ref, pos_ref, i, E, SUB_BF16)
    first, end, zero = _segment_cols(segs)
    sel = sel_ref[...]
    aff = aff_ref[...]
    a_hi = aff.astype(BF16).astype(F32)
    a_mid = (aff - a_hi).astype(BF16).astype(F32)
    a_lo = ((aff - a_hi) - a_mid).astype(BF16).astype(F32)
    rank = jnp.dot(sel.astype(BF16), _strict_upper(tm), preferred_element_type=F32)
    rank = jnp.where(sel > 0, rank, NO_RANK)
    pad = jnp.zeros((LANES - (OWNER_COPIES + 1) * E, tm), F32)
    by_token = jnp.transpose(jnp.concatenate([a_hi, a_mid, a_lo, rank, pad], axis=0)).astype(BF16)
    o_ref[...] = x_ref[...]
    _wait_pieces(_total_pieces(segs), lambda rows: piece_copy(slot, 0, 0, 0, rows))

    def chunk(k, c):
        j0 = pl.multiple_of(k * CHUNK, CHUNK)
        jl =(j0 + lax.broadcasted_iota(jnp.int32, (E, CHUNK), 1)).astype(F32)
        own = (jl >= first) & (jl < end)
        tgt = jnp.sum(jnp.where(own, jl - zero + NO_OWNER, 0.0), axis=0, keepdims=True) - NO_OWNER
        o16 = jnp.where(own, 1.0, 0.0).astype(BF16)
        zeros = lambda n: jnp.zeros((n, CHUNK), BF16)
        own_aff = jnp.concatenate([o16] * OWNER_COPIES + [zeros(LANES - OWNER_COPIES * E)], axis=0)
        own_rank = jnp.concatenate([zeros(OWNER_COPIES * E), o16, zeros(LANES - (OWNER_COPIES + 1) * E)], axis=0)
        hit = jnp.dot(by_token, own_rank, preferred_element_type=F32) == tgt
        g = jnp.where(hit, jnp.dot(by_token, own_aff, preferred_element_type=F32), 0.0)
        g_hi = g.astype(BF16)
        g_lo = (g - g_hi.astype(F32)).astype(BF16)
        y = stack[slot, pl.ds(j0, CHUNK), :]
        o_ref[...] += (jnp.dot(g_hi, y, preferred_element_type=F32)
                       + jnp.dot(g_lo, y, preferred_element_type=F32))
        return c

    lax.fori_loop(0, _cdiv_pow2(total, LOG2_CHUNK), chunk, 0)
    if g_ref is not None:
        o_ref[...] = _rms(o_ref[...], g_ref[...])


def _combine(sel, aff, cnt, pos, x, y, final_g=None):
    NT, E, tm = sel.shape
    T, D = x.shape
    tile_spec = pl.BlockSpec((None, E, tm), lambda i, c, p: (i, 0, 0))
    x_spec = pl.BlockSpec((tm, D), lambda i, c, p: (i, 0))
    extra_specs = [] if final_g is None else [pl.BlockSpec((1, D), lambda i, c, p: (0, 0))]
    extra = [] if final_g is None else [final_g]
    return pl.pallas_call(
        _combine_kernel,
        grid_spec=pltpu.PrefetchScalarGridSpec(
            num_scalar_prefetch=2,
            grid=(NT,),
            in_specs=[tile_spec, tile_spec, x_spec, pl.BlockSpec(memory_space=pl.ANY)] + extra_specs,
            out_specs=x_spec,
            scratch_shapes=[pltpu.VMEM((2, _stack_rows(E, tm, SUB_BF16), D), BF16),
                            pltpu.SemaphoreType.DMA((2,))]),
        out_shape=jax.ShapeDtypeStruct((T, D), F32),
        compiler_params=_params(1),
        name="expert_combine" if final_g is None else "expert_combine_norm",
    )(cnt, pos, sel, aff, x, y, *extra)


def _ffn_kernel(x_ref, wg_ref, wu_ref, wd_ref, y_ref):
    i = pl.program_id(1)
    last = pl.num_programs(1) - 1

    @pl.when(i < last)
    def _():
        x = x_ref[...].astype(BF16)
        a = jnp.dot(x, wg_ref[...], preferred_element_type=F32)
        b = jnp.dot(x, wu_ref[...], preferred_element_type=F32)
        h = (a * (1.0 / (1.0 + jnp.exp(-a)))) * b
        y_ref[...] = jnp.dot(h.astype(BF16), wd_ref[...], preferred_element_type=F32).astype(y_ref.dtype)

    @pl.when(i == last)
    def _():
        y_ref[...] = jnp.zeros_like(y_ref)


def _expert_ffn(xe, cap, wg, wu, wd, layer):
    E, _, D = xe.shape
    F = wg.shape[3]
    tm = min(ROW_TILE, cap)
    n = cap // tm
    return pl.pallas_call(
        _ffn_kernel,
        grid=(E, n + 1),
        in_specs=[
            pl.BlockSpec((None, tm, D), lambda e, i: (e, jnp.minimum(i, n - 1), 0)),
            pl.BlockSpec((None, None, D, F), lambda e, i: (layer, e, 0, 0)),
            pl.BlockSpec((None, None, D, F), lambda e, i: (layer, e, 0, 0)),
            pl.BlockSpec((None, None, F, D), lambda e, i: (layer, e, 0, 0)),
        ],
        out_specs=pl.BlockSpec((None, tm, D), lambda e, i: (e, i, 0)),
        out_shape=jax.ShapeDtypeStruct((E, cap + tm, D), BF16),
        compiler_params=_params(2),
        name=f"expert_ffn_cap{cap}",
    )(xe, wg, wu, wd)


def _moe(x, hn, aff, wg, wu, wd, layer, final_g):
    T, D = x.shape
    cap = max(1, (EC_CAPACITY * T) // N_EXPERTS)
    sel, cnt, pos = _select(aff, cap)
    xe = _dispatch(sel, cnt, pos, hn, cap)
    y = _expert_ffn(xe, cap, wg, wu, wd, layer)
    return _combine(sel, aff, cnt, pos, x, y, final_g)


def _trunk(x3, p):
    B, S, D = x3.shape
    x = x3.reshape(B * S, D)
    tables = _rope_tables(S)
    depth = p["ffn_norm"].shape[0]
    for i in range(depth):
        j = i // 2
        g_ffn, w_rt = p["ffn_norm"][i][None], p["w_router_t"][i]
        if i % 2 == 0:
            qkvs = _qkv_proj(x, p["attn_norm"][j][None], p["w_qkv"][j], tables, B, S)
            res = [_dilated_attention(qkvs[g], g) for g in range(N_GROUPS)]
            x, hn, aff = _attn_out_proj([r[0] for r in res], [r[1] for r in res], p["w_attn_out"][j], x,
                                        g_ffn, w_rt, S)
        else:
            x, hn, aff = _short_conv(x, p["conv_norm"][j][None], p["w_conv_in"][j], p["conv_w"][j],
                                     p["w_conv_out"][j], g_ffn, w_rt, S)
        final_g = p["final_norm"][None] if i == depth - 1 else None
        x = _moe(x, hn, aff, p["w_gate"], p["w_up"], p["w_down"], i, final_g)
    return x.reshape(B, S, D)


def kernel(x_prompt, x_sample, attn_norm, w_qkv, w_attn_out, conv_norm, w_conv_in, conv_w, w_conv_out,
           ffn_norm, w_router, w_gate, w_up, w_down, final_norm):
    per_layer = lambda w: [w[i].astype(BF16) for i in range(w.shape[0])]
    p = dict(
        attn_norm=attn_norm, conv_norm=conv_norm, ffn_norm=ffn_norm, final_norm=final_norm, conv_w=conv_w,
        w_qkv=per_layer(w_qkv), w_attn_out=per_layer(w_attn_out),
        w_conv_in=per_layer(w_conv_in), w_conv_out=per_layer(w_conv_out),
        w_router_t=per_layer(jnp.swapaxes(w_router, 1, 2)),
        w_gate=w_gate.astype(BF16), w_up=w_up.astype(BF16), w_down=w_down.astype(BF16),
    )
    return (_trunk(x_prompt, p), _trunk(x_sample, p))
```

```python
import functools

import jax
import jax.numpy as jnp
from jax import lax
from jax.experimental import pallas as pl
from jax.experimental.pallas import tpu as pltpu

D_MODEL = 1024
HEAD_DIM = 64
HEADS_PER_GROUP = 4
DILATED_GROUPS = ((128, 1), (512, 4), (2048, 16))
N_GROUPS = len(DILATED_GROUPS)
ATTN_WIDTH = N_GROUPS * HEADS_PER_GROUP * HEAD_DIM
GROUP_WIDTH = HEADS_PER_GROUP * HEAD_DIM
ROPE_DIM = HEAD_DIM // 4
ROPE_THETA = 500000.0
N_EXPERTS = 16
EC_CAPACITY = 2
RMS_EPS = 1e-6
NEG_INF = -1e30

LANES = 128
SUB = 8
SUB_BF16 = 16
ROW_TILE = 512
ATTN_TQ = 128
MOE_TILE = 256
CHUNK = 256
PIECE = 16
LOG2_PIECE = 4
LOG2_CHUNK = 8
LOG2_SUB = 3
assert (1 << LOG2_PIECE, 1 << LOG2_CHUNK, 1 << LOG2_SUB) == (PIECE, CHUNK, SUB)
VMEM_LIMIT = 56 * 1024 * 1024

F32 = jnp.float32
BF16 = jnp.bfloat16


def _params(n_axes):
    return pltpu.CompilerParams(
        dimension_semantics=("arbitrary",) * n_axes, vmem_limit_bytes=VMEM_LIMIT)


def _rms(x, g):
    ms = jnp.mean(x * x, axis=-1, keepdims=True)
    return x * lax.rsqrt(ms + RMS_EPS) * g


def _qkv_kernel(x_ref, g_ref, w_ref, cos_ref, sa_ref, sb_ref, o0_ref, o1_ref, o2_ref, s_ref):
    tm = x_ref.shape[0]
    h = _rms(x_ref[...], g_ref[...]).astype(BF16)
    acc = jnp.dot(h, w_ref[...], preferred_element_type=F32)
    c, sa, sb = cos_ref[...], sa_ref[...], sb_ref[...]
    n_q = ATTN_WIDTH // LANES
    for j in range(2 * n_q):
        blk = acc[:, j * LANES:(j + 1) * LANES]
        r = blk * c + pltpu.roll(blk, LANES - ROPE_DIM // 2, 1) * sa + pltpu.roll(blk, ROPE_DIM // 2, 1) * sb
        if j < n_q:
            r = r * (HEAD_DIM ** -0.5)
        s_ref[j] = r
    for j in range(2 * n_q, 3 * n_q):
        s_ref[j] = acc[:, j * LANES:(j + 1) * LANES]
    per_group = GROUP_WIDTH // LANES
    for g, o_ref in enumerate((o0_ref, o1_ref, o2_ref)):
        dil = DILATED_GROUPS[g][1]
        for part in range(3):
            for jj in range(per_group):
                src = part * n_q + g * per_group + jj
                dst = slice((part * per_group + jj) * LANES, (part * per_group + jj + 1) * LANES)
                for r in range(dil):
                    rows = pl.ds(r, tm // dil, stride=dil) if dil > 1 else slice(None)
                    o_ref[r, :, dst] = s_ref[src, rows, :].astype(BF16)


def _rope_tables(S):
    half = ROPE_DIM // 2
    inv_freq = jnp.power(ROPE_THETA, -jnp.arange(0, ROPE_DIM, 2, dtype=F32) / ROPE_DIM)
    ang = jnp.arange(S, dtype=F32)[:, None] * inv_freq[None, :]
    cos, sin = jnp.cos(ang), jnp.sin(ang)
    rest = HEAD_DIM - ROPE_DIM
    c = jnp.concatenate([cos, cos, jnp.ones((S, rest), F32)], axis=1)
    sa = jnp.concatenate([-sin, jnp.zeros((S, half + rest), F32)], axis=1)
    sb = jnp.concatenate([jnp.zeros((S, half), F32), sin, jnp.zeros((S, rest), F32)], axis=1)
    rep = LANES // HEAD_DIM
    return tuple(jnp.tile(t, (1, rep)) for t in (c, sa, sb))


def _qkv_proj(x, g, w, tables, B, S):
    T, D = x.shape
    N = w.shape[1]
    tm = ROW_TILE
    per_seq = S // tm
    tab_spec = pl.BlockSpec((tm, LANES), lambda i: (i % per_seq, 0))
    dils = [d for _, d in DILATED_GROUPS]
    return pl.pallas_call(
        _qkv_kernel,
        grid=(T // tm,),
        in_specs=[
            pl.BlockSpec((tm, D), lambda i: (i, 0)),
            pl.BlockSpec((1, D), lambda i: (0, 0)),
            pl.BlockSpec((D, N), lambda i: (0, 0)),
            tab_spec, tab_spec, tab_spec,
        ],
        out_specs=[pl.BlockSpec((None, d, tm // d, 3 * GROUP_WIDTH), lambda i: (i // per_seq, 0, i % per_seq, 0))
                   for d in dils],
        out_shape=[jax.ShapeDtypeStruct((B, d, S // d, 3 * GROUP_WIDTH), BF16) for d in dils],
        scratch_shapes=[pltpu.VMEM((N // LANES, tm, LANES), F32)],
        compiler_params=_params(1),
        name="qkv_proj",
    )(x, g, w, *tables)


def _attn_kernel(q_ref, k_ref, v_ref, o_ref, l_ref, *, L, Lq, kw, tq, radius):
    c = pl.program_id(2)

    def body(i, carry):
        r0 = pl.multiple_of(i * tq, tq)
        t0 = c * Lq + i * tq
        ks = pl.multiple_of(jnp.clip(t0 - radius, 0, L - kw), radius)
        qb = q_ref[pl.ds(r0, tq), :]
        kb = k_ref[pl.ds(ks, kw), :]
        vb = v_ref[pl.ds(ks, kw), :]
        qi = t0 + lax.broadcasted_iota(jnp.int32, (tq, kw), 0)
        kj = ks + lax.broadcasted_iota(jnp.int32, (tq, kw), 1)
        valid = jnp.abs(qi - kj) <= radius
        outs, lses = [], []
        for h in range(HEADS_PER_GROUP):
            sl = slice(h * HEAD_DIM, (h + 1) * HEAD_DIM)
            s = lax.dot_general(qb[:, sl], kb[:, sl], (((1,), (1,)), ((), ())),
                                preferred_element_type=F32)
            s = jnp.where(valid, s, NEG_INF)
            m = jnp.max(s, axis=-1, keepdims=True)
            p = jnp.exp(s - m)
            l = jnp.sum(p, axis=-1, keepdims=True)
            o = jnp.dot(p.astype(BF16), vb[:, sl], preferred_element_type=F32) / l
            outs.append(o)
            lses.append(jnp.broadcast_to(m + jnp.log(l), (tq, HEAD_DIM)))
        o_ref[pl.ds(r0, tq), :] = jnp.concatenate(outs, axis=1)
        l_ref[pl.ds(r0, tq), :] = jnp.concatenate(lses, axis=1)
        return carry

    n_blk = Lq // tq
    lax.fori_loop(0, n_blk, body, 0, unroll=2 if n_blk % 2 == 0 else 1)


def _dilated_attention(qkv, group):
    window, dil = DILATED_GROUPS[group]
    B, _, L, _ = qkv.shape
    radius = window // (2 * dil)
    tq = min(ATTN_TQ, L)
    kw = min(L, tq + 2 * radius)
    Lq = min(L, 1024)
    assert L % tq == 0 and L % Lq == 0 and Lq % tq == 0 and radius % 16 == 0 and (L - kw) % radius == 0
    kern = functools.partial(_attn_kernel, L=L, Lq=Lq, kw=kw, tq=tq, radius=radius)
    out_spec = pl.BlockSpec((None, None, Lq, GROUP_WIDTH), lambda b, r, c: (b, r, c, 0))
    return pl.pallas_call(
        kern,
        grid=(B, dil, L // Lq),
        in_specs=[
            pl.BlockSpec((None, None, Lq, GROUP_WIDTH), lambda b, r, c: (b, r, c, 0)),
            pl.BlockSpec((None, None, L, GROUP_WIDTH), lambda b, r, c: (b, r, 0, 1)),
            pl.BlockSpec((None, None, L, GROUP_WIDTH), lambda b, r, c: (b, r, 0, 2)),
        ],
        out_specs=[out_spec, out_spec],
        out_shape=[jax.ShapeDtypeStruct((B, dil, L, GROUP_WIDTH), F32)] * 2,
        compiler_params=_params(3),
        name=f"dilated_attn_g{group}",
    )(qkv, qkv, qkv)


def _oproj_kernel(o0_ref, o1_ref, o2_ref, l0_ref, l1_ref, l2_ref, w_ref, x_ref, gf_ref, wr_ref,
                  y_ref, hn_ref, aff_ref, o_scr, l_scr):
    tm = x_ref.shape[0]
    for g, (o_ref, l_ref) in enumerate(((o0_ref, l0_ref), (o1_ref, l1_ref), (o2_ref, l2_ref))):
        dil = DILATED_GROUPS[g][1]
        for r in range(dil):
            rows = pl.ds(r, tm // dil, stride=dil) if dil > 1 else slice(None)
            for jj in range(GROUP_WIDTH // LANES):
                lanes = slice(jj * LANES, (jj + 1) * LANES)
                o_scr[g, jj, rows, :] = o_ref[r, :, lanes]
                l_scr[g, jj, rows, :] = l_ref[r, :, lanes]
    zs = []
    for jj in range(GROUP_WIDTH // LANES):
        l0, l1, l2 = l_scr[0, jj], l_scr[1, jj], l_scr[2, jj]
        m = jnp.maximum(jnp.maximum(l0, l1), l2)
        e0, e1, e2 = jnp.exp(l0 - m), jnp.exp(l1 - m), jnp.exp(l2 - m)
        den = e0 + e1 + e2
        zs.append((o_scr[0, jj] * (e0 / den), o_scr[1, jj] * (e1 / den), o_scr[2, jj] * (e2 / den)))
    z = jnp.concatenate([zs[jj][g] for g in range(N_GROUPS) for jj in range(GROUP_WIDTH // LANES)], axis=1)
    y = x_ref[...] + jnp.dot(z.astype(BF16), w_ref[...], preferred_element_type=F32)
    y_ref[...] = y
    _route(y, gf_ref, wr_ref, hn_ref, aff_ref)


def _attn_out_proj(outs, lses, w, x, g_ffn, w_rt, S):
    T, D = x.shape
    tm = ROW_TILE
    per_seq = S // tm
    gspecs = [pl.BlockSpec((None, d, tm // d, GROUP_WIDTH), lambda i: (i // per_seq, 0, i % per_seq, 0))
              for _, d in DILATED_GROUPS]
    xspec = pl.BlockSpec((tm, D), lambda i: (i, 0))
    r_in, r_out, r_shapes = _route_specs(T, D, w_rt.shape[0], tm)
    return pl.pallas_call(
        _oproj_kernel,
        grid=(T // tm,),
        in_specs=gspecs + gspecs + [pl.BlockSpec((ATTN_WIDTH, D), lambda i: (0, 0)), xspec] + r_in,
        out_specs=[xspec] + r_out,
        out_shape=[jax.ShapeDtypeStruct((T, D), F32)] + r_shapes,
        scratch_shapes=[pltpu.VMEM((N_GROUPS, GROUP_WIDTH // LANES, tm, LANES), F32)] * 2,
        compiler_params=_params(1),
        name="attn_out_proj",
    )(*outs, *lses, w, x, g_ffn, w_rt)


def _route(x, gf_ref, wr_ref, hn_ref, aff_ref):
    h = _rms(x, gf_ref[...]).astype(BF16)
    hn_ref[...] = h
    logits = lax.dot_general(wr_ref[...], h, (((1,), (1,)), ((), ())),
                             preferred_element_type=F32)
    m = jnp.max(logits, axis=0, keepdims=True)
    e = jnp.exp(logits - m)
    aff = e / jnp.sum(e, axis=0, keepdims=True)
    for k in range(aff_ref.shape[0]):
        aff_ref[k] = aff[:, k * MOE_TILE:(k + 1) * MOE_TILE]


def _route_specs(T, D, E, tm):
    per = tm // MOE_TILE
    in_specs = [pl.BlockSpec((1, D), lambda i: (0, 0)), pl.BlockSpec((E, D), lambda i: (0, 0))]
    out_specs = [pl.BlockSpec((tm, D), lambda i: (i, 0)), pl.BlockSpec((per, E, MOE_TILE), lambda i: (i, 0, 0))]
    out_shapes = [jax.ShapeDtypeStruct((T, D), BF16), jax.ShapeDtypeStruct((T // MOE_TILE, E, MOE_TILE), F32)]
    return in_specs, out_specs, out_shapes


def _conv_kernel(x_ref, xp_ref, xn_ref, g_ref, win_ref, cw_ref, wout_ref, gf_ref, wr_ref,
                 y_ref, hn_ref, aff_ref, u_scr, *, tm, per_seq):
    i = pl.program_id(0)
    D = x_ref.shape[1]
    x = x_ref[...]
    xa = jnp.concatenate([xp_ref[...], x, xn_ref[...]], axis=0)
    h = _rms(xa, g_ref[...]).astype(BF16)
    full = jnp.dot(h, win_ref[...], preferred_element_type=F32)
    u_scr[...] = full[:, D:2 * D] * full[:, 2 * D:]
    first = (i % per_seq) == 0
    last = (i % per_seq) == per_seq - 1
    u_scr[pl.ds(7, 1), :] = jnp.where(first, 0.0, u_scr[pl.ds(7, 1), :])
    u_scr[pl.ds(tm + 8, 1), :] = jnp.where(last, 0.0, u_scr[pl.ds(tm + 8, 1), :])
    cw = cw_ref[...]
    conv = (cw[0:1] * u_scr[pl.ds(7, tm), :] + cw[1:2] * u_scr[pl.ds(8, tm), :]
            + cw[2:3] * u_scr[pl.ds(9, tm), :])
    z = (full[8:8 + tm, :D] * conv).astype(BF16)
    y = x + jnp.dot(z, wout_ref[...], preferred_element_type=F32)
    y_ref[...] = y
    _route(y, gf_ref, wr_ref, hn_ref, aff_ref)


def _short_conv(x, g, w_in, cw, w_out, g_ffn, w_rt, S):
    T, D = x.shape
    tm = ROW_TILE
    per_seq = S // tm
    sub = tm // 8
    n8 = T // 8
    kern = functools.partial(_conv_kernel, tm=tm, per_seq=per_seq)
    r_in, r_out, r_shapes = _route_specs(T, D, w_rt.shape[0], tm)
    return pl.pallas_call(
        kern,
        grid=(T // tm,),
        in_specs=[
            pl.BlockSpec((tm, D), lambda i: (i, 0)),
            pl.BlockSpec((8, D), lambda i: (jnp.maximum(i * sub - 1, 0), 0)),
            pl.BlockSpec((8, D), lambda i: (jnp.minimum((i + 1) * sub, n8 - 1), 0)),
            pl.BlockSpec((1, D), lambda i: (0, 0)),
            pl.BlockSpec((D, 3 * D), lambda i: (0, 0)),
            pl.BlockSpec((3, D), lambda i: (0, 0)),
            pl.BlockSpec((D, D), lambda i: (0, 0)),
        ] + r_in,
        out_specs=[pl.BlockSpec((tm, D), lambda i: (i, 0))] + r_out,
        out_shape=[jax.ShapeDtypeStruct((T, D), F32)] + r_shapes,
        scratch_shapes=[pltpu.VMEM((tm + 16, D), F32)],
        compiler_params=_params(1),
        name="short_conv",
    )(x, x, x, g, w_in, cw, w_out, g_ffn, w_rt)


def _strict_upper(n):
    r = lax.broadcasted_iota(jnp.int32, (n, n), 0)
    c = lax.broadcasted_iota(jnp.int32, (n, n), 1)
    return (r < c).astype(BF16)


def _select_kernel(aff_ref, sel_ref, cnt_ref, pos_ref, *, cap):
    NT, E, tm = aff_ref.shape
    capf = float(cap)

    def count_ge(v):
        bits = lax.bitcast_convert_type(aff_ref[...], jnp.int32)
        ge = (bits >= v[None]).astype(F32)
        return jnp.sum(jnp.sum(ge, axis=0), axis=1, keepdims=True)

    def bisect(_, lh):
        lo, hi = lh
        mid = lo + lax.shift_right_logical(hi - lo, 1)
        ok = count_ge(mid) >= capf
        return jnp.where(ok, mid, lo), jnp.where(ok, hi, mid)

    one_bits = 0x3F800000
    lo0 = jnp.zeros((E, 1), jnp.int32)
    hi0 = jnp.full((E, 1), one_bits + 1, jnp.int32)
    tau, _ = lax.fori_loop(0, 31, bisect, (lo0, hi0))
    need = capf - count_ge(tau + 1)
    tri = _strict_upper(tm)

    def tile(j, carry):
        ceq, csel = carry
        bits = lax.bitcast_convert_type(aff_ref[j], jnp.int32)
        gt = bits > tau
        eq = (bits == tau).astype(F32)
        rank_eq = jnp.dot(eq.astype(BF16), tri, preferred_element_type=F32) + ceq
        sel = jnp.where(gt | ((eq > 0) & (rank_eq < need)), 1.0, 0.0)
        sel_ref[j] = sel
        c = jnp.sum(sel, axis=1, keepdims=True)
        cnt_ref[j] = jnp.broadcast_to(c, (E, LANES))
        pos_ref[j] = jnp.broadcast_to(csel, (E, LANES))
        return ceq + jnp.sum(eq, axis=1, keepdims=True), csel + c

    zero = jnp.zeros((E, 1), F32)
    lax.fori_loop(0, NT, tile, (zero, zero))


def _select(aff, cap):
    NT, E, tm = aff.shape
    sel, cnt, pos = pl.pallas_call(
        functools.partial(_select_kernel, cap=cap),
        out_shape=[jax.ShapeDtypeStruct((NT, E, tm), F32),
                   jax.ShapeDtypeStruct((NT, E, LANES), F32),
                   jax.ShapeDtypeStruct((NT, E, LANES), F32)],
        compiler_params=pltpu.CompilerParams(vmem_limit_bytes=VMEM_LIMIT),
        name="expert_select",
    )(aff)
    to_smem = lambda a: a[:, :, 0].astype(jnp.int32).reshape(NT * E)
    return sel, to_smem(cnt), to_smem(pos)


def _stack_rows(E, tm, sub):
    worst = E * tm + E * (sub - 1 + PIECE - 1)
    return -(-worst // CHUNK) * CHUNK


def _cdiv_pow2(x, log2):
    return lax.shift_right_logical(x + ((1 << log2) - 1), log2)


def _segments(cnt_ref, pos_ref, tile, E, sub):
    segs = []
    base = jnp.int32(0)
    for e in range(E):
        n = cnt_ref[tile * E + e]
        p0 = pos_ref[tile * E + e]
        a = p0 & (sub - 1)
        pieces = jnp.where(n > 0, _cdiv_pow2(a + n, LOG2_PIECE), 0)
        segs.append((base, pieces, p0 - a, a, n))
        base = base + pieces * PIECE
    return segs, base


NO_RANK = -256.0
NO_OWNER = 1024.0
OWNER_COPIES = 3


def _segment_rows(segs):
    E = len(segs)
    lane = lax.broadcasted_iota(jnp.int32, (1, LANES), 1)
    used = lane < OWNER_COPIES * E
    first = jnp.full((1, LANES), float(NO_OWNER * NO_OWNER), F32)
    end = jnp.full((1, LANES), -1.0, F32)
    zero = jnp.zeros((1, LANES), F32)
    for e, (base, pieces, _, a, _) in enumerate(segs):
        m = used & ((lane & (E - 1)) == e)
        first = jnp.where(m, base.astype(F32), first)
        end = jnp.where(m, (base + pieces * PIECE).astype(F32), end)
        zero = jnp.where(m, (base + a).astype(F32), zero)
    return first, end, zero


def _tile_ranks(sel, tm):
    E = sel.shape[0]
    rank = jnp.dot(sel.astype(BF16), _strict_upper(tm), preferred_element_type=F32)
    rank = jnp.where(sel > 0, rank, NO_RANK).astype(BF16)
    return jnp.concatenate([rank, jnp.zeros((LANES - E, tm), BF16)], axis=0)


def _chunk_one_hot(j0, rows, ranks, E):
    first, end, zero = rows
    jv = (j0 + lax.broadcasted_iota(jnp.int32, (CHUNK, LANES), 0)).astype(F32)
    own = (jv >= first) & (jv < end)
    lane = lax.broadcasted_iota(jnp.int32, (CHUNK, LANES), 1)
    tgt = jnp.sum(jnp.where(own & (lane < E), jv - zero + NO_OWNER, 0.0), axis=1, keepdims=True) - NO_OWNER
    ownb = jnp.where(own, 1.0, 0.0).astype(BF16)
    hit = jnp.dot(ownb, ranks, preferred_element_type=F32) == tgt
    return ownb, hit


def _total_pieces(segs):
    t = segs[0][1]
    for s in segs[1:]:
        t = t + s[1]
    return t


WAIT_GROUP = 8
LOG2_WAIT_GROUP = 3
assert 1 << LOG2_WAIT_GROUP == WAIT_GROUP


def _wait_pieces(n, copy_of_rows):
    def waits(count, rows):
        def w(_, c):
            copy_of_rows(rows).wait()
            return c
        lax.fori_loop(0, count, w, 0)
    waits(lax.shift_right_logical(n, LOG2_WAIT_GROUP), WAIT_GROUP * PIECE)
    waits(n & (WAIT_GROUP - 1), PIECE)


def _dispatch_kernel(cnt_ref, pos_ref, sel_ref, hn_ref, xe_ref, stack, pend, sem):
    i = pl.program_id(0)
    nt = pl.num_programs(0)
    slot = i % 2
    E, tm = sel_ref.shape

    @pl.when(i == 0)
    def _():
        pend[...] = jnp.zeros_like(pend)

    segs, total = _segments(cnt_ref, pos_ref, i, E, SUB)
    rows = _segment_rows(segs)
    ranks = _tile_ranks(sel_ref[...], tm)

    def chunk(k, c):
        j0 = pl.multiple_of(k * CHUNK, CHUNK)
        _, hit = _chunk_one_hot(j0, rows, ranks, E)
        oh = jnp.where(hit, 1.0, 0.0).astype(BF16)
        stack[slot, pl.ds(j0, CHUNK), :] = jnp.dot(oh, hn_ref[...], preferred_element_type=F32)
        return c

    lax.fori_loop(0, _cdiv_pow2(total, LOG2_CHUNK), chunk, 0)

    for e in range(E):
        base, _, _, a, n = segs[e]

        @pl.when(n > 0)
        def _():
            first = pl.multiple_of(base, PIECE)
            stack[slot, pl.ds(first, SUB), :] = stack[slot, pl.ds(first, SUB), :] + pend[e]
            last = pl.multiple_of(base + lax.shift_right_logical(a + n - 1, LOG2_SUB) * SUB, SUB)
            incomplete = ((a + n) & (SUB - 1)) != 0
            pend[e] = jnp.where(incomplete, stack[slot, pl.ds(last, SUB), :], 0.0)

    def piece_copy(s, src_row, e, dst_row, rows=PIECE):
        return pltpu.make_async_copy(stack.at[s, pl.ds(src_row, rows), :],
                                     xe_ref.at[e, pl.ds(dst_row, rows), :], sem.at[s])

    @pl.when(i > 0)
    def _():
        prev, _ = _segments(cnt_ref, pos_ref, i - 1, E, SUB)
        _wait_pieces(_total_pieces(prev), lambda rows: piece_copy(1 - slot, 0, 0, 0, rows))

    for e in range(E):
        base, pieces, start, _, _ = segs[e]

        def issue(c, carry, base=base, start=start, e=e):
            piece_copy(slot, pl.multiple_of(base + c * PIECE, PIECE), e,
                       pl.multiple_of(start + c * PIECE, SUB)).start()
            return carry

        lax.fori_loop(0, pieces, issue, 0)

    @pl.when(i == nt - 1)
    def _():
        _wait_pieces(_total_pieces(segs), lambda rows: piece_copy(slot, 0, 0, 0, rows))
        cap = xe_ref.shape[1] - PIECE
        stack[slot, pl.ds(0, PIECE), :] = jnp.zeros((PIECE, stack.shape[2]), F32)
        for e in range(E):
            piece_copy(slot, 0, e, cap).start()
        _wait_pieces(E, lambda rows: piece_copy(slot, 0, 0, 0, rows))


def _dispatch(sel, cnt, pos, hn, cap):
    NT, E, tm = sel.shape
    T, D = hn.shape
    return pl.pallas_call(
        _dispatch_kernel,
        grid_spec=pltpu.PrefetchScalarGridSpec(
            num_scalar_prefetch=2,
            grid=(NT,),
            in_specs=[pl.BlockSpec((None, E, tm), lambda i, c, p: (i, 0, 0)),
                      pl.BlockSpec((tm, D), lambda i, c, p: (i, 0))],
            out_specs=pl.BlockSpec(memory_space=pl.ANY),
            scratch_shapes=[pltpu.VMEM((2, _stack_rows(E, tm, SUB), D), F32),
                            pltpu.VMEM((E, SUB, D), F32),
                            pltpu.SemaphoreType.DMA((2,))]),
        out_shape=jax.ShapeDtypeStruct((E, cap + PIECE, D), F32),
        compiler_params=_params(1),
        name="expert_dispatch",
    )(cnt, pos, sel, hn)


def _segment_cols(segs):
    E = len(segs)
    eidx = lax.broadcasted_iota(jnp.int32, (E, 1), 0)
    first = jnp.zeros((E, 1), F32)
    end = jnp.zeros((E, 1), F32)
    zero = jnp.zeros((E, 1), F32)
    for e, (base, pieces, _, a, _) in enumerate(segs):
        m = eidx == e
        first = jnp.where(m, base.astype(F32), first)
        end = jnp.where(m, (base + pieces * PIECE).astype(F32), end)
        zero = jnp.where(m, (base + a).astype(F32), zero)
    return first, end, zero


def _combine_kernel(cnt_ref, pos_ref, sel_ref, aff_ref, x_ref, y_hbm, *rest):
    g_ref = rest[0] if len(rest) == 4 else None
    o_ref, stack, sem = rest[-3:]
    i = pl.program_id(0)
    nt = pl.num_programs(0)
    slot = i % 2
    E, tm = sel_ref.shape

    def piece_copy(s, e, src_row, dst_row, rows=PIECE):
        return pltpu.make_async_copy(y_hbm.at[e, pl.ds(src_row, rows), :],
                                     stack.at[s, pl.ds(dst_row, rows), :], sem.at[s])

    def fetch(tile, s):
        segs, _ = _segments(cnt_ref, pos_ref, tile, E, SUB_BF16)
        for e in range(E):
            base, pieces, start, _, _ = segs[e]

            def issue(c, carry, base=base, start=start, e=e):
                piece_copy(s, e, pl.multiple_of(start + c * PIECE, SUB_BF16),
                           pl.multiple_of(base + c * PIECE, PIECE)).start()
                return carry

            lax.fori_loop(0, pieces, issue, 0)

    @pl.when(i == 0)
    def _():
        stack[...] = jnp.zeros_like(stack)
        fetch(0, 0)

    @pl.when(i + 1 < nt)
    def _():
        fetch(i + 1, 1 - slot)

    segs, total = _segments(cnt_ref, pos_ref, i, E, SUB_BF16)
    first, end, zero = _segment_cols(segs)
    sel = sel_ref[...]
    aff = aff_ref[...]
    a_hi = aff.astype(BF16).astype(F32)
    a_mid = (aff - a_hi).astype(BF16).astype(F32)
    a_lo = ((aff - a_hi) - a_mid).astype(BF16).astype(F32)
    rank = jnp.dot(sel.astype(BF16), _strict_upper(tm), preferred_element_type=F32)
    rank = jnp.where(sel > 0, rank, NO_RANK)
    pad = jnp.zeros((LANES - (OWNER_COPIES + 1) * E, tm), F32)
    by_token = jnp.transpose(jnp.concatenate([a_hi, a_mid, a_lo, rank, pad], axis=0)).astype(BF16)
    o_ref[...] = x_ref[...]
    _wait_pieces(_total_pieces(segs), lambda rows: piece_copy(slot, 0, 0, 0, rows))

    def chunk(k, c):
        j0 = pl.multiple_of(k * CHUNK, CHUNK)
        jl =(j0 + lax.broadcasted_iota(jnp.int32, (E, CHUNK), 1)).astype(F32)
        own = (jl >= first) & (jl < end)
        tgt = jnp.sum(jnp.where(own, jl - zero + NO_OWNER, 0.0), axis=0, keepdims=True) - NO_OWNER
        o16 = jnp.where(own, 1.0, 0.0).astype(BF16)
        zeros = lambda n: jnp.zeros((n, CHUNK), BF16)
        own_aff = jnp.concatenate([o16] * OWNER_COPIES + [zeros(LANES - OWNER_COPIES * E)], axis=0)
        own_rank = jnp.concatenate([zeros(OWNER_COPIES * E), o16, zeros(LANES - (OWNER_COPIES + 1) * E)], axis=0)
        hit = jnp.dot(by_token, own_rank, preferred_element_type=F32) == tgt
        g = jnp.where(hit, jnp.dot(by_token, own_aff, preferred_element_type=F32), 0.0)
        g_hi = g.astype(BF16)
        g_lo = (g - g_hi.astype(F32)).astype(BF16)
        y = stack[slot, pl.ds(j0, CHUNK), :]
        o_ref[...] += (jnp.dot(g_hi, y, preferred_element_type=F32)
                       + jnp.dot(g_lo, y, preferred_element_type=F32))
        return c

    lax.fori_loop(0, _cdiv_pow2(total, LOG2_CHUNK), chunk, 0)
    if g_ref is not None:
        o_ref[...] = _rms(o_ref[...], g_ref[...])


def _combine(sel, aff, cnt, pos, x, y, final_g=None):
    NT, E, tm = sel.shape
    T, D = x.shape
    tile_spec = pl.BlockSpec((None, E, tm), lambda i, c, p: (i, 0, 0))
    x_spec = pl.BlockSpec((tm, D), lambda i, c, p: (i, 0))
    extra_specs = [] if final_g is None else [pl.BlockSpec((1, D), lambda i, c, p: (0, 0))]
    extra = [] if final_g is None else [final_g]
    return pl.pallas_call(
        _combine_kernel,
        grid_spec=pltpu.PrefetchScalarGridSpec(
            num_scalar_prefetch=2,
            grid=(NT,),
            in_specs=[tile_spec, tile_spec, x_spec, pl.BlockSpec(memory_space=pl.ANY)] + extra_specs,
            out_specs=x_spec,
            scratch_shapes=[pltpu.VMEM((2, _stack_rows(E, tm, SUB_BF16), D), BF16),
                            pltpu.SemaphoreType.DMA((2,))]),
        out_shape=jax.ShapeDtypeStruct((T, D), F32),
        compiler_params=_params(1),
        name="expert_combine" if final_g is None else "expert_combine_norm",
    )(cnt, pos, sel, aff, x, y, *extra)


def _ffn_kernel(x_ref, wg_hbm, wu_hbm, wd_hbm, y_ref, wg_buf, wu_buf, wd_buf, wg_stg, wu_stg, wd_stg, sem,
                *, layer):
    e = pl.program_id(0)
    i = pl.program_id(1)
    n_e = pl.num_programs(0)
    n = pl.num_programs(1) - 1
    slot = e % 2
    stages = ((wg_hbm, wg_stg, wg_buf), (wu_hbm, wu_stg, wu_buf), (wd_hbm, wd_stg, wd_buf))

    def chunk_copy(k, expert, c):
        hbm, stg, _ = stages[k]
        rows = stg.shape[0]
        return pltpu.make_async_copy(hbm.at[layer, expert, pl.ds(pl.multiple_of(c * rows, rows), rows), :],
                                     stg, sem.at[k])

    def land(expert, c, dst_slot):
        for k, (_, stg, buf) in enumerate(stages):
            rows = stg.shape[0]
            chunk_copy(k, expert, c).wait()
            buf[dst_slot, pl.ds(pl.multiple_of(c * rows, rows), rows), :] = stg[...].astype(BF16)

    @pl.when((e == 0) & (i == 0))
    def _():
        def first(c, carry):
            for k in range(3):
                chunk_copy(k, 0, c).start()
            land(0, c, 0)
            return carry
        lax.fori_loop(0, n, first, 0)

    nxt = jnp.minimum(e + 1, n_e - 1)

    @pl.when((e + 1 < n_e) & (i > 0))
    def _():
        land(nxt, i - 1, 1 - slot)

    @pl.when((e + 1 < n_e) & (i < n))
    def _():
        for k in range(3):
            chunk_copy(k, nxt, i).start()

    @pl.when(i < n)
    def _():
        x = x_ref[...].astype(BF16)
        a = jnp.dot(x, wg_buf[slot], preferred_element_type=F32)
        b = jnp.dot(x, wu_buf[slot], preferred_element_type=F32)
        h = (a * (1.0 / (1.0 + jnp.exp(-a)))) * b
        y_ref[...] = jnp.dot(h.astype(BF16), wd_buf[slot], preferred_element_type=F32).astype(y_ref.dtype)

    @pl.when(i == n)
    def _():
        y_ref[...] = jnp.zeros_like(y_ref)


def _expert_ffn(xe, cap, wg, wu, wd, layer):
    E, _, D = xe.shape
    F = wg.shape[3]
    tm = min(ROW_TILE, cap)
    n = cap // tm
    any_spec = pl.BlockSpec(memory_space=pl.ANY)
    return pl.pallas_call(
        functools.partial(_ffn_kernel, layer=layer),
        grid=(E, n + 1),
        in_specs=[pl.BlockSpec((None, tm, D), lambda e, i: (e, jnp.minimum(i, n - 1), 0)),
                  any_spec, any_spec, any_spec],
        out_specs=pl.BlockSpec((None, tm, D), lambda e, i: (e, i, 0)),
        out_shape=jax.ShapeDtypeStruct((E, cap + tm, D), BF16),
        scratch_shapes=[pltpu.VMEM((2, D, F), BF16), pltpu.VMEM((2, D, F), BF16), pltpu.VMEM((2, F, D), BF16),
                        pltpu.VMEM((D // n, F), F32), pltpu.VMEM((D // n, F), F32), pltpu.VMEM((F // n, D), F32),
                        pltpu.SemaphoreType.DMA((3,))],
        compiler_params=_params(2),
        name=f"expert_ffn_cap{cap}",
    )(xe, wg, wu, wd)


def _moe(x, hn, aff, wg, wu, wd, layer, final_g):
    T, D = x.shape
    cap = max(1, (EC_CAPACITY * T) // N_EXPERTS)
    sel, cnt, pos = _select(aff, cap)
    xe = _dispatch(sel, cnt, pos, hn, cap)
    y = _expert_ffn(xe, cap, wg, wu, wd, layer)
    return _combine(sel, aff, cnt, pos, x, y, final_g)


def _trunk(x3, p):
    B, S, D = x3.shape
    x = x3.reshape(B * S, D)
    tables = _rope_tables(S)
    depth = p["ffn_norm"].shape[0]
    for i in range(depth):
        j = i // 2
        g_ffn, w_rt = p["ffn_norm"][i][None], p["w_router_t"][i]
        if i % 2 == 0:
            qkvs = _qkv_proj(x, p["attn_norm"][j][None], p["w_qkv"][j], tables, B, S)
            res = [_dilated_attention(qkvs[g], g) for g in range(N_GROUPS)]
            x, hn, aff = _attn_out_proj([r[0] for r in res], [r[1] for r in res], p["w_attn_out"][j], x,
                                        g_ffn, w_rt, S)
        else:
            x, hn, aff = _short_conv(x, p["conv_norm"][j][None], p["w_conv_in"][j], p["conv_w"][j],
                                     p["w_conv_out"][j], g_ffn, w_rt, S)
        final_g = p["final_norm"][None] if i == depth - 1 else None
        x = _moe(x, hn, aff, p["w_gate"], p["w_up"], p["w_down"], i, final_g)
    return x.reshape(B, S, D)


def kernel(x_prompt, x_sample, attn_norm, w_qkv, w_attn_out, conv_norm, w_conv_in, conv_w, w_conv_out,
           ffn_norm, w_router, w_gate, w_up, w_down, final_norm):
    per_layer = lambda w: [w[i].astype(BF16) for i in range(w.shape[0])]
    p = dict(
        attn_norm=attn_norm, conv_norm=conv_norm, ffn_norm=ffn_norm, final_norm=final_norm, conv_w=conv_w,
        w_qkv=per_layer(w_qkv), w_attn_out=per_layer(w_attn_out),
        w_conv_in=per_layer(w_conv_in), w_conv_out=per_layer(w_conv_out),
        w_router_t=per_layer(jnp.swapaxes(w_router, 1, 2)),
        w_gate=w_gate, w_up=w_up, w_down=w_down,
    )
    return (_trunk(x_prompt, p), _trunk(x_sample, p))
```

```python
import functools

import jax
import jax.numpy as jnp
from jax import lax
from jax.experimental import pallas as pl
from jax.experimental.pallas import tpu as pltpu

D_MODEL = 1024
HEAD_DIM = 64
HEADS_PER_GROUP = 4
DILATED_GROUPS = ((128, 1), (512, 4), (2048, 16))
N_GROUPS = len(DILATED_GROUPS)
ATTN_WIDTH = N_GROUPS * HEADS_PER_GROUP * HEAD_DIM
GROUP_WIDTH = HEADS_PER_GROUP * HEAD_DIM
ROPE_DIM = HEAD_DIM // 4
ROPE_THETA = 500000.0
N_EXPERTS = 16
EC_CAPACITY = 2
RMS_EPS = 1e-6
NEG_INF = -1e30

LANES = 128
SUB = 8
SUB_BF16 = 16
ROW_TILE = 512
ATTN_TQ = 128
MOE_TILE = 256
CHUNK = 256
PIECE = 16
WIDE = 64
LOG2_PIECE = 4
LOG2_CHUNK = 8
LOG2_SUB = 3
LOG2_WIDE_PIECES = 2
assert (1 << LOG2_PIECE, 1 << LOG2_CHUNK, 1 << LOG2_SUB) == (PIECE, CHUNK, SUB)
assert PIECE << LOG2_WIDE_PIECES == WIDE
VMEM_LIMIT = 56 * 1024 * 1024

F32 = jnp.float32
BF16 = jnp.bfloat16


def _params(n_axes):
    return pltpu.CompilerParams(
        dimension_semantics=("arbitrary",) * n_axes, vmem_limit_bytes=VMEM_LIMIT)


def _rms(x, g):
    ms = jnp.mean(x * x, axis=-1, keepdims=True)
    return x * lax.rsqrt(ms + RMS_EPS) * g


def _qkv_kernel(x_ref, g_ref, w_ref, cos_ref, sa_ref, sb_ref, o0_ref, o1_ref, o2_ref, s_ref):
    tm = x_ref.shape[0]
    h = _rms(x_ref[...], g_ref[...]).astype(BF16)
    acc = jnp.dot(h, w_ref[...], preferred_element_type=F32)
    c, sa, sb = cos_ref[...], sa_ref[...], sb_ref[...]
    n_q = ATTN_WIDTH // LANES
    for j in range(2 * n_q):
        blk = acc[:, j * LANES:(j + 1) * LANES]
        r = blk * c + pltpu.roll(blk, LANES - ROPE_DIM // 2, 1) * sa + pltpu.roll(blk, ROPE_DIM // 2, 1) * sb
        if j < n_q:
            r = r * (HEAD_DIM ** -0.5)
        s_ref[j] = r
    for j in range(2 * n_q, 3 * n_q):
        s_ref[j] = acc[:, j * LANES:(j + 1) * LANES]
    per_group = GROUP_WIDTH // LANES
    for g, o_ref in enumerate((o0_ref, o1_ref, o2_ref)):
        dil = DILATED_GROUPS[g][1]
        for part in range(3):
            for jj in range(per_group):
                src = part * n_q + g * per_group + jj
                dst = slice((part * per_group + jj) * LANES, (part * per_group + jj + 1) * LANES)
                for r in range(dil):
                    rows = pl.ds(r, tm // dil, stride=dil) if dil > 1 else slice(None)
                    o_ref[r, :, dst] = s_ref[src, rows, :].astype(BF16)


def _rope_tables(S):
    half = ROPE_DIM // 2
    inv_freq = jnp.power(ROPE_THETA, -jnp.arange(0, ROPE_DIM, 2, dtype=F32) / ROPE_DIM)
    ang = jnp.arange(S, dtype=F32)[:, None] * inv_freq[None, :]
    cos, sin = jnp.cos(ang), jnp.sin(ang)
    rest = HEAD_DIM - ROPE_DIM
    c = jnp.concatenate([cos, cos, jnp.ones((S, rest), F32)], axis=1)
    sa = jnp.concatenate([-sin, jnp.zeros((S, half + rest), F32)], axis=1)
    sb = jnp.concatenate([jnp.zeros((S, half), F32), sin, jnp.zeros((S, rest), F32)], axis=1)
    rep = LANES // HEAD_DIM
    return tuple(jnp.tile(t, (1, rep)) for t in (c, sa, sb))


def _qkv_proj(x, g, w, tables, B, S):
    T, D = x.shape
    N = w.shape[1]
    tm = ROW_TILE
    per_seq = S // tm
    tab_spec = pl.BlockSpec((tm, LANES), lambda i: (i % per_seq, 0))
    dils = [d for _, d in DILATED_GROUPS]
    return pl.pallas_call(
        _qkv_kernel,
        grid=(T // tm,),
        in_specs=[
            pl.BlockSpec((tm, D), lambda i: (i, 0)),
            pl.BlockSpec((1, D), lambda i: (0, 0)),
            pl.BlockSpec((D, N), lambda i: (0, 0)),
            tab_spec, tab_spec, tab_spec,
        ],
        out_specs=[pl.BlockSpec((None, d, tm // d, 3 * GROUP_WIDTH), lambda i: (i // per_seq, 0, i % per_seq, 0))
                   for d in dils],
        out_shape=[jax.ShapeDtypeStruct((B, d, S // d, 3 * GROUP_WIDTH), BF16) for d in dils],
        scratch_shapes=[pltpu.VMEM((N // LANES, tm, LANES), F32)],
        compiler_params=_params(1),
        name="qkv_proj",
    )(x, g, w, *tables)


def _attn_kernel(q_ref, k_ref, v_ref, o_ref, l_ref, *, L, Lq, kw, tq, radius):
    c = pl.program_id(2)

    def body(i, carry):
        r0 = pl.multiple_of(i * tq, tq)
        t0 = c * Lq + i * tq
        ks = pl.multiple_of(jnp.clip(t0 - radius, 0, L - kw), radius)
        qb = q_ref[pl.ds(r0, tq), :]
        kb = k_ref[pl.ds(ks, kw), :]
        vb = v_ref[pl.ds(ks, kw), :]
        qi = t0 + lax.broadcasted_iota(jnp.int32, (tq, kw), 0)
        kj = ks + lax.broadcasted_iota(jnp.int32, (tq, kw), 1)
        valid = jnp.abs(qi - kj) <= radius
        per_block = LANES // HEAD_DIM
        lane_head = lax.broadcasted_iota(jnp.int32, (1, LANES), 1) // HEAD_DIM
        for blk in range(GROUP_WIDTH // LANES):
            lanes = slice(blk * LANES, (blk + 1) * LANES)
            qp, kp, vp = qb[:, lanes], kb[:, lanes], vb[:, lanes]
            o_blk = jnp.zeros((tq, LANES), F32)
            lse_blk = jnp.zeros((tq, LANES), F32)
            for hh in range(per_block):
                mine = lane_head == hh
                s = lax.dot_general(jnp.where(mine, qp, jnp.zeros_like(qp)), kp, (((1,), (1,)), ((), ())),
                                    preferred_element_type=F32)
                s = jnp.where(valid, s, NEG_INF)
                m = jnp.max(s, axis=-1, keepdims=True)
                p = jnp.exp(s - m)
                l = jnp.sum(p, axis=-1, keepdims=True)
                o_blk = o_blk + jnp.dot(p.astype(BF16), jnp.where(mine, vp, jnp.zeros_like(vp)),
                                        preferred_element_type=F32) / l
                lse_blk = jnp.where(mine, m + jnp.log(l), lse_blk)
            o_ref[pl.ds(r0, tq), lanes] = o_blk
            l_ref[pl.ds(r0, tq), lanes] = lse_blk
        return carry

    n_blk = Lq // tq
    lax.fori_loop(0, n_blk, body, 0, unroll=2 if n_blk % 2 == 0 else 1)


def _dilated_attention(qkv, group):
    window, dil = DILATED_GROUPS[group]
    B, _, L, _ = qkv.shape
    radius = window // (2 * dil)
    tq = min(ATTN_TQ, L)
    kw = min(L, tq + 2 * radius)
    Lq = min(L, 1024)
    assert L % tq == 0 and L % Lq == 0 and Lq % tq == 0 and radius % 16 == 0 and (L - kw) % radius == 0
    kern = functools.partial(_attn_kernel, L=L, Lq=Lq, kw=kw, tq=tq, radius=radius)
    out_spec = pl.BlockSpec((None, None, Lq, GROUP_WIDTH), lambda b, r, c: (b, r, c, 0))
    return pl.pallas_call(
        kern,
        grid=(B, dil, L // Lq),
        in_specs=[
            pl.BlockSpec((None, None, Lq, GROUP_WIDTH), lambda b, r, c: (b, r, c, 0)),
            pl.BlockSpec((None, None, L, GROUP_WIDTH), lambda b, r, c: (b, r, 0, 1)),
            pl.BlockSpec((None, None, L, GROUP_WIDTH), lambda b, r, c: (b, r, 0, 2)),
        ],
        out_specs=[out_spec, out_spec],
        out_shape=[jax.ShapeDtypeStruct((B, dil, L, GROUP_WIDTH), F32)] * 2,
        compiler_params=_params(3),
        name=f"dilated_attn_g{group}",
    )(qkv, qkv, qkv)


def _oproj_kernel(o0_ref, o1_ref, o2_ref, l0_ref, l1_ref, l2_ref, w_ref, x_ref, gf_ref, wr_ref,
                  y_ref, hn_ref, aff_ref, o_scr, l_scr):
    tm = x_ref.shape[0]
    for g, (o_ref, l_ref) in enumerate(((o0_ref, l0_ref), (o1_ref, l1_ref), (o2_ref, l2_ref))):
        dil = DILATED_GROUPS[g][1]
        for r in range(dil):
            rows = pl.ds(r, tm // dil, stride=dil) if dil > 1 else slice(None)
            for jj in range(GROUP_WIDTH // LANES):
                lanes = slice(jj * LANES, (jj + 1) * LANES)
                o_scr[g, jj, rows, :] = o_ref[r, :, lanes]
                l_scr[g, jj, rows, :] = l_ref[r, :, lanes]
    zs = []
    for jj in range(GROUP_WIDTH // LANES):
        l0, l1, l2 = l_scr[0, jj], l_scr[1, jj], l_scr[2, jj]
        m = jnp.maximum(jnp.maximum(l0, l1), l2)
        e0, e1, e2 = jnp.exp(l0 - m), jnp.exp(l1 - m), jnp.exp(l2 - m)
        den = e0 + e1 + e2
        zs.append((o_scr[0, jj] * (e0 / den), o_scr[1, jj] * (e1 / den), o_scr[2, jj] * (e2 / den)))
    z = jnp.concatenate([zs[jj][g] for g in range(N_GROUPS) for jj in range(GROUP_WIDTH // LANES)], axis=1)
    y = x_ref[...] + jnp.dot(z.astype(BF16), w_ref[...], preferred_element_type=F32)
    y_ref[...] = y
    _route(y, gf_ref, wr_ref, hn_ref, aff_ref)


def _attn_out_proj(outs, lses, w, x, g_ffn, w_rt, S):
    T, D = x.shape
    tm = ROW_TILE
    per_seq = S // tm
    gspecs = [pl.BlockSpec((None, d, tm // d, GROUP_WIDTH), lambda i: (i // per_seq, 0, i % per_seq, 0))
              for _, d in DILATED_GROUPS]
    xspec = pl.BlockSpec((tm, D), lambda i: (i, 0))
    r_in, r_out, r_shapes = _route_specs(T, D, w_rt.shape[0], tm)
    return pl.pallas_call(
        _oproj_kernel,
        grid=(T // tm,),
        in_specs=gspecs + gspecs + [pl.BlockSpec((ATTN_WIDTH, D), lambda i: (0, 0)), xspec] + r_in,
        out_specs=[xspec] + r_out,
        out_shape=[jax.ShapeDtypeStruct((T, D), F32)] + r_shapes,
        scratch_shapes=[pltpu.VMEM((N_GROUPS, GROUP_WIDTH // LANES, tm, LANES), F32)] * 2,
        compiler_params=_params(1),
        name="attn_out_proj",
    )(*outs, *lses, w, x, g_ffn, w_rt)


def _route(x, gf_ref, wr_ref, hn_ref, aff_ref):
    h = _rms(x, gf_ref[...]).astype(BF16)
    hn_ref[...] = h
    logits = lax.dot_general(wr_ref[...], h, (((1,), (1,)), ((), ())),
                             preferred_element_type=F32)
    m = jnp.max(logits, axis=0, keepdims=True)
    e = jnp.exp(logits - m)
    aff = e / jnp.sum(e, axis=0, keepdims=True)
    for k in range(aff_ref.shape[0]):
        aff_ref[k] = aff[:, k * MOE_TILE:(k + 1) * MOE_TILE]


def _route_specs(T, D, E, tm):
    per = tm // MOE_TILE
    in_specs = [pl.BlockSpec((1, D), lambda i: (0, 0)), pl.BlockSpec((E, D), lambda i: (0, 0))]
    out_specs = [pl.BlockSpec((tm, D), lambda i: (i, 0)), pl.BlockSpec((per, E, MOE_TILE), lambda i: (i, 0, 0))]
    out_shapes = [jax.ShapeDtypeStruct((T, D), BF16), jax.ShapeDtypeStruct((T // MOE_TILE, E, MOE_TILE), F32)]
    return in_specs, out_specs, out_shapes


def _conv_kernel(x_ref, xp_ref, xn_ref, g_ref, win_ref, cw_ref, wout_ref, gf_ref, wr_ref,
                 y_ref, hn_ref, aff_ref, u_scr, *, tm, per_seq):
    i = pl.program_id(0)
    D = x_ref.shape[1]
    x = x_ref[...]
    xa = jnp.concatenate([xp_ref[...], x, xn_ref[...]], axis=0)
    h = _rms(xa, g_ref[...]).astype(BF16)
    full = jnp.dot(h, win_ref[...], preferred_element_type=F32)
    u_scr[...] = full[:, D:2 * D] * full[:, 2 * D:]
    first = (i % per_seq) == 0
    last = (i % per_seq) == per_seq - 1
    u_scr[pl.ds(7, 1), :] = jnp.where(first, 0.0, u_scr[pl.ds(7, 1), :])
    u_scr[pl.ds(tm + 8, 1), :] = jnp.where(last, 0.0, u_scr[pl.ds(tm + 8, 1), :])
    cw = cw_ref[...]
    conv = (cw[0:1] * u_scr[pl.ds(7, tm), :] + cw[1:2] * u_scr[pl.ds(8, tm), :]
            + cw[2:3] * u_scr[pl.ds(9, tm), :])
    z = (full[8:8 + tm, :D] * conv).astype(BF16)
    y = x + jnp.dot(z, wout_ref[...], preferred_element_type=F32)
    y_ref[...] = y
    _route(y, gf_ref, wr_ref, hn_ref, aff_ref)


def _short_conv(x, g, w_in, cw, w_out, g_ffn, w_rt, S):
    T, D = x.shape
    tm = ROW_TILE
    per_seq = S // tm
    sub = tm // 8
    n8 = T // 8
    kern = functools.partial(_conv_kernel, tm=tm, per_seq=per_seq)
    r_in, r_out, r_shapes = _route_specs(T, D, w_rt.shape[0], tm)
    return pl.pallas_call(
        kern,
        grid=(T // tm,),
        in_specs=[
            pl.BlockSpec((tm, D), lambda i: (i, 0)),
            pl.BlockSpec((8, D), lambda i: (jnp.maximum(i * sub - 1, 0), 0)),
            pl.BlockSpec((8, D), lambda i: (jnp.minimum((i + 1) * sub, n8 - 1), 0)),
            pl.BlockSpec((1, D), lambda i: (0, 0)),
            pl.BlockSpec((D, 3 * D), lambda i: (0, 0)),
            pl.BlockSpec((3, D), lambda i: (0, 0)),
            pl.BlockSpec((D, D), lambda i: (0, 0)),
        ] + r_in,
        out_specs=[pl.BlockSpec((tm, D), lambda i: (i, 0))] + r_out,
        out_shape=[jax.ShapeDtypeStruct((T, D), F32)] + r_shapes,
        scratch_shapes=[pltpu.VMEM((tm + 16, D), F32)],
        compiler_params=_params(1),
        name="short_conv",
    )(x, x, x, g, w_in, cw, w_out, g_ffn, w_rt)


def _strict_upper(n):
    r = lax.broadcasted_iota(jnp.int32, (n, n), 0)
    c = lax.broadcasted_iota(jnp.int32, (n, n), 1)
    return (r < c).astype(BF16)


def _select_kernel(aff_ref, sel_ref, cnt_ref, pos_ref, *, cap):
    NT, E, tm = aff_ref.shape
    capf = float(cap)

    def count_ge(v):
        bits = lax.bitcast_convert_type(aff_ref[...], jnp.int32)
        ge = (bits >= v[None]).astype(F32)
        return jnp.sum(jnp.sum(ge, axis=0), axis=1, keepdims=True)

    def bisect(_, lh):
        lo, hi = lh
        mid = lo + lax.shift_right_logical(hi - lo, 1)
        ok = count_ge(mid) >= capf
        return jnp.where(ok, mid, lo), jnp.where(ok, hi, mid)

    one_bits = 0x3F800000
    lo0 = jnp.zeros((E, 1), jnp.int32)
    hi0 = jnp.full((E, 1), one_bits + 1, jnp.int32)
    tau, _ = lax.fori_loop(0, 31, bisect, (lo0, hi0))
    need = capf - count_ge(tau + 1)
    tri = _strict_upper(tm)

    def tile(j, carry):
        ceq, csel = carry
        bits = lax.bitcast_convert_type(aff_ref[j], jnp.int32)
        gt = bits > tau
        eq = (bits == tau).astype(F32)
        rank_eq = jnp.dot(eq.astype(BF16), tri, preferred_element_type=F32) + ceq
        sel = jnp.where(gt | ((eq > 0) & (rank_eq < need)), 1.0, 0.0)
        sel_ref[j] = sel
        c = jnp.sum(sel, axis=1, keepdims=True)
        cnt_ref[j] = jnp.broadcast_to(c, (E, LANES))
        pos_ref[j] = jnp.broadcast_to(csel, (E, LANES))
        return ceq + jnp.sum(eq, axis=1, keepdims=True), csel + c

    zero = jnp.zeros((E, 1), F32)
    lax.fori_loop(0, NT, tile, (zero, zero))


def _select(aff, cap):
    NT, E, tm = aff.shape
    sel, cnt, pos = pl.pallas_call(
        functools.partial(_select_kernel, cap=cap),
        out_shape=[jax.ShapeDtypeStruct((NT, E, tm), F32),
                   jax.ShapeDtypeStruct((NT, E, LANES), F32),
                   jax.ShapeDtypeStruct((NT, E, LANES), F32)],
        compiler_params=pltpu.CompilerParams(vmem_limit_bytes=VMEM_LIMIT),
        name="expert_select",
    )(aff)
    to_smem = lambda a: a[:, :, 0].astype(jnp.int32).reshape(NT * E)
    return sel, to_smem(cnt), to_smem(pos)


def _stack_rows(E, tm, sub):
    worst = E * tm + E * (sub - 1 + PIECE - 1)
    return -(-worst // CHUNK) * CHUNK


def _cdiv_pow2(x, log2):
    return lax.shift_right_logical(x + ((1 << log2) - 1), log2)


def _segments(cnt_ref, pos_ref, tile, E, sub):
    segs = []
    base = jnp.int32(0)
    for e in range(E):
        n = cnt_ref[tile * E + e]
        p0 = pos_ref[tile * E + e]
        a = p0 & (sub - 1)
        pieces = jnp.where(n > 0, _cdiv_pow2(a + n, LOG2_PIECE), 0)
        segs.append((base, pieces, p0 - a, a, n))
        base = base + pieces * PIECE
    return segs, base


NO_RANK = -256.0
NO_OWNER = 1024.0
OWNER_COPIES = 3


def _segment_rows(segs):
    E = len(segs)
    lane = lax.broadcasted_iota(jnp.int32, (1, LANES), 1)
    used = lane < OWNER_COPIES * E
    first = jnp.full((1, LANES), float(NO_OWNER * NO_OWNER), F32)
    end = jnp.full((1, LANES), -1.0, F32)
    zero = jnp.zeros((1, LANES), F32)
    for e, (base, pieces, _, a, _) in enumerate(segs):
        m = used & ((lane & (E - 1)) == e)
        first = jnp.where(m, base.astype(F32), first)
        end = jnp.where(m, (base + pieces * PIECE).astype(F32), end)
        zero = jnp.where(m, (base + a).astype(F32), zero)
    return first, end, zero


def _tile_ranks(sel, tm):
    E = sel.shape[0]
    rank = jnp.dot(sel.astype(BF16), _strict_upper(tm), preferred_element_type=F32)
    rank = jnp.where(sel > 0, rank, NO_RANK).astype(BF16)
    return jnp.concatenate([rank, jnp.zeros((LANES - E, tm), BF16)], axis=0)


def _chunk_one_hot(j0, rows, ranks, E):
    first, end, zero = rows
    jv = (j0 + lax.broadcasted_iota(jnp.int32, (CHUNK, LANES), 0)).astype(F32)
    own = (jv >= first) & (jv < end)
    lane = lax.broadcasted_iota(jnp.int32, (CHUNK, LANES), 1)
    tgt = jnp.sum(jnp.where(own & (lane < E), jv - zero + NO_OWNER, 0.0), axis=1, keepdims=True) - NO_OWNER
    ownb = jnp.where(own, 1.0, 0.0).astype(BF16)
    hit = jnp.dot(ownb, ranks, preferred_element_type=F32) == tgt
    return ownb, hit


def _total_pieces(segs):
    t = segs[0][1]
    for s in segs[1:]:
        t = t + s[1]
    return t


WAIT_GROUP = 8
LOG2_WAIT_GROUP = 3
assert 1 << LOG2_WAIT_GROUP == WAIT_GROUP


def _wait_pieces(n, copy_of_rows, unit=PIECE):
    def waits(count, rows):
        def w(_, c):
            copy_of_rows(rows).wait()
            return c
        lax.fori_loop(0, count, w, 0)
    waits(lax.shift_right_logical(n, LOG2_WAIT_GROUP), WAIT_GROUP * unit)
    waits(n & (WAIT_GROUP - 1), unit)


def _dispatch_kernel(cnt_ref, pos_ref, sel_ref, hn_ref, xe_ref, stack, pend, sem):
    i = pl.program_id(0)
    nt = pl.num_programs(0)
    slot = i % 2
    E, tm = sel_ref.shape

    @pl.when(i == 0)
    def _():
        pend[...] = jnp.zeros_like(pend)
        stack[...] = jnp.zeros_like(stack)

    segs, total = _segments(cnt_ref, pos_ref, i, E, SUB)
    rows = _segment_rows(segs)
    ranks = _tile_ranks(sel_ref[...], tm)

    def chunk(k, c):
        j0 = pl.multiple_of(k * CHUNK, CHUNK)
        _, hit = _chunk_one_hot(j0, rows, ranks, E)
        oh = jnp.where(hit, 1.0, 0.0).astype(BF16)
        stack[slot, pl.ds(j0, CHUNK), :] = jnp.dot(oh, hn_ref[...], preferred_element_type=F32)
        return c

    lax.fori_loop(0, _cdiv_pow2(total, LOG2_CHUNK), chunk, 0)

    for e in range(E):
        base, _, _, a, n = segs[e]

        @pl.when(n > 0)
        def _():
            first = pl.multiple_of(base, PIECE)
            stack[slot, pl.ds(first, SUB), :] = stack[slot, pl.ds(first, SUB), :] + pend[e]
            last = pl.multiple_of(base + lax.shift_right_logical(a + n - 1, LOG2_SUB) * SUB, SUB)
            incomplete = ((a + n) & (SUB - 1)) != 0
            pend[e] = jnp.where(incomplete, stack[slot, pl.ds(last, SUB), :], 0.0)

    def wide_copy(s, src_row, e, dst_row, rows=WIDE):
        return pltpu.make_async_copy(stack.at[s, pl.ds(src_row, rows), :],
                                     xe_ref.at[e, pl.ds(dst_row, rows), :], sem.at[s])

    def copies_of(some_segs):
        t = jnp.int32(0)
        for _, pieces, _, _, _ in some_segs:
            t = t + _cdiv_pow2(pieces, LOG2_WIDE_PIECES)
        return t

    waiter = lambda s: (lambda rows: wide_copy(s, 0, 0, 0, rows))

    @pl.when(i > 0)
    def _():
        prev, _ = _segments(cnt_ref, pos_ref, i - 1, E, SUB)
        _wait_pieces(copies_of(prev), waiter(1 - slot), WIDE)

    for e in range(E):
        base, pieces, start, _, _ = segs[e]

        def issue(c, carry, base=base, start=start, e=e):
            wide_copy(slot, pl.multiple_of(base + c * WIDE, PIECE), e,
                      pl.multiple_of(start + c * WIDE, SUB)).start()
            return carry

        lax.fori_loop(0, _cdiv_pow2(pieces, LOG2_WIDE_PIECES), issue, 0)

    @pl.when(i == nt - 1)
    def _():
        _wait_pieces(copies_of(segs), waiter(slot), WIDE)
        cap = xe_ref.shape[1] - WIDE
        stack[slot, pl.ds(0, WIDE), :] = jnp.zeros((WIDE, stack.shape[2]), F32)
        for e in range(E):
            wide_copy(slot, 0, e, cap).start()
        _wait_pieces(E, waiter(slot), WIDE)


def _dispatch(sel, cnt, pos, hn, cap):
    NT, E, tm = sel.shape
    T, D = hn.shape
    return pl.pallas_call(
        _dispatch_kernel,
        grid_spec=pltpu.PrefetchScalarGridSpec(
            num_scalar_prefetch=2,
            grid=(NT,),
            in_specs=[pl.BlockSpec((None, E, tm), lambda i, c, p: (i, 0, 0)),
                      pl.BlockSpec((tm, D), lambda i, c, p: (i, 0))],
            out_specs=pl.BlockSpec(memory_space=pl.ANY),
            scratch_shapes=[pltpu.VMEM((2, _stack_rows(E, tm, SUB) + WIDE, D), F32),
                            pltpu.VMEM((E, SUB, D), F32),
                            pltpu.SemaphoreType.DMA((2,))]),
        out_shape=jax.ShapeDtypeStruct((E, cap + WIDE, D), F32),
        compiler_params=_params(1),
        name="expert_dispatch",
    )(cnt, pos, sel, hn)


def _segment_cols(segs):
    E = len(segs)
    eidx = lax.broadcasted_iota(jnp.int32, (E, 1), 0)
    first = jnp.zeros((E, 1), F32)
    end = jnp.zeros((E, 1), F32)
    zero = jnp.zeros((E, 1), F32)
    for e, (base, pieces, _, a, _) in enumerate(segs):
        m = eidx == e
        first = jnp.where(m, base.astype(F32), first)
        end = jnp.where(m, (base + pieces * PIECE).astype(F32), end)
        zero = jnp.where(m, (base + a).astype(F32), zero)
    return first, end, zero


def _combine_kernel(cnt_ref, pos_ref, sel_ref, aff_ref, x_ref, y_hbm, *rest):
    g_ref = rest[0] if len(rest) == 4 else None
    o_ref, stack, sem = rest[-3:]
    i = pl.program_id(0)
    nt = pl.num_programs(0)
    slot = i % 2
    E, tm = sel_ref.shape

    def piece_copy(s, e, src_row, dst_row, rows=PIECE):
        return pltpu.make_async_copy(y_hbm.at[e, pl.ds(src_row, rows), :],
                                     stack.at[s, pl.ds(dst_row, rows), :], sem.at[s])

    def fetch(tile, s):
        segs, _ = _segments(cnt_ref, pos_ref, tile, E, SUB_BF16)
        for e in range(E):
            base, pieces, start, _, _ = segs[e]

            def issue(c, carry, base=base, start=start, e=e):
                piece_copy(s, e, pl.multiple_of(start + c * PIECE, SUB_BF16),
                           pl.multiple_of(base + c * PIECE, PIECE)).start()
                return carry

            lax.fori_loop(0, pieces, issue, 0)

    @pl.when(i == 0)
    def _():
        stack[...] = jnp.zeros_like(stack)
        fetch(0, 0)

    @pl.when(i + 1 < nt)
    def _():
        fetch(i + 1, 1 - slot)

    segs, total = _segments(cnt_ref, pos_ref, i, E, SUB_BF16)
    first, end, zero = _segment_cols(segs)
    sel = sel_ref[...]
    aff = aff_ref[...]
    a_hi = aff.astype(BF16).astype(F32)
    a_mid = (aff - a_hi).astype(BF16).astype(F32)
    a_lo = ((aff - a_hi) - a_mid).astype(BF16).astype(F32)
    rank = jnp.dot(sel.astype(BF16), _strict_upper(tm), preferred_element_type=F32)
    rank = jnp.where(sel > 0, rank, NO_RANK)
    pad = jnp.zeros((LANES - (OWNER_COPIES + 1) * E, tm), F32)
    by_token = jnp.transpose(jnp.concatenate([a_hi, a_mid, a_lo, rank, pad], axis=0)).astype(BF16)
    o_ref[...] = x_ref[...]
    _wait_pieces(_total_pieces(segs), lambda rows: piece_copy(slot, 0, 0, 0, rows))

    def chunk(k, c):
        j0 = pl.multiple_of(k * CHUNK, CHUNK)
        jl = (j0 + lax.broadcasted_iota(jnp.int32, (E, CHUNK), 1)).astype(F32)
        own = (jl >= first) & (jl < end)
        tgt = jnp.sum(jnp.where(own, jl - zero + NO_OWNER, 0.0), axis=0, keepdims=True) - NO_OWNER
        o16 = jnp.where(own, 1.0, 0.0).astype(BF16)
        zeros = lambda n: jnp.zeros((n, CHUNK), BF16)
        own_aff = jnp.concatenate([o16] * OWNER_COPIES + [zeros(LANES - OWNER_COPIES * E)], axis=0)
        own_rank = jnp.concatenate([zeros(OWNER_COPIES * E), o16, zeros(LANES - (OWNER_COPIES + 1) * E)], axis=0)
        hit = jnp.dot(by_token, own_rank, preferred_element_type=F32) == tgt
        g = jnp.where(hit, jnp.dot(by_token, own_aff, preferred_element_type=F32), 0.0)
        g_hi = g.astype(BF16)
        g_lo = (g - g_hi.astype(F32)).astype(BF16)
        y = stack[slot, pl.ds(j0, CHUNK), :]
        o_ref[...] += (jnp.dot(g_hi, y, preferred_element_type=F32)
                       + jnp.dot(g_lo, y, preferred_element_type=F32))
        return c

    lax.fori_loop(0, _cdiv_pow2(total, LOG2_CHUNK), chunk, 0)
    if g_ref is not None:
        o_ref[...] = _rms(o_ref[...], g_ref[...])


def _combine(sel, aff, cnt, pos, x, y, final_g=None):
    NT, E, tm = sel.shape
    T, D = x.shape
    tile_spec = pl.BlockSpec((None, E, tm), lambda i, c, p: (i, 0, 0))
    x_spec = pl.BlockSpec((tm, D), lambda i, c, p: (i, 0))
    extra_specs = [] if final_g is None else [pl.BlockSpec((1, D), lambda i, c, p: (0, 0))]
    extra = [] if final_g is None else [final_g]
    return pl.pallas_call(
        _combine_kernel,
        grid_spec=pltpu.PrefetchScalarGridSpec(
            num_scalar_prefetch=2,
            grid=(NT,),
            in_specs=[tile_spec, tile_spec, x_spec, pl.BlockSpec(memory_space=pl.ANY)] + extra_specs,
            out_specs=x_spec,
            scratch_shapes=[pltpu.VMEM((2, _stack_rows(E, tm, SUB_BF16), D), BF16),
                            pltpu.SemaphoreType.DMA((2,))]),
        out_shape=jax.ShapeDtypeStruct((T, D), F32),
        compiler_params=_params(1),
        name="expert_combine" if final_g is None else "expert_combine_norm",
    )(cnt, pos, sel, aff, x, y, *extra)


def _ffn_kernel(x_ref, wg_hbm, wu_hbm, wd_hbm, y_ref, wg_buf, wu_buf, wd_buf, wg_stg, wu_stg, wd_stg, sem,
                *, layer):
    e = pl.program_id(0)
    i = pl.program_id(1)
    n_e = pl.num_programs(0)
    n = pl.num_programs(1) - 1
    slot = e % 2
    stages = ((wg_hbm, wg_stg, wg_buf), (wu_hbm, wu_stg, wu_buf), (wd_hbm, wd_stg, wd_buf))

    def chunk_copy(k, expert, c):
        hbm, stg, _ = stages[k]
        rows = stg.shape[0]
        return pltpu.make_async_copy(hbm.at[layer, expert, pl.ds(pl.multiple_of(c * rows, rows), rows), :],
                                     stg, sem.at[k])

    def land(expert, c, dst_slot):
        for k, (_, stg, buf) in enumerate(stages):
            rows = stg.shape[0]
            chunk_copy(k, expert, c).wait()
            buf[dst_slot, pl.ds(pl.multiple_of(c * rows, rows), rows), :] = stg[...].astype(BF16)

    @pl.when((e == 0) & (i == 0))
    def _():
        def first(c, carry):
            for k in range(3):
                chunk_copy(k, 0, c).start()
            land(0, c, 0)
            return carry
        lax.fori_loop(0, n, first, 0)

    nxt = jnp.minimum(e + 1, n_e - 1)

    @pl.when((e + 1 < n_e) & (i > 0))
    def _():
        land(nxt, i - 1, 1 - slot)

    @pl.when((e + 1 < n_e) & (i < n))
    def _():
        for k in range(3):
            chunk_copy(k, nxt, i).start()

    @pl.when(i < n)
    def _():
        x = x_ref[...].astype(BF16)
        a = jnp.dot(x, wg_buf[slot], preferred_element_type=F32)
        b = jnp.dot(x, wu_buf[slot], preferred_element_type=F32)
        h = (a * (1.0 / (1.0 + jnp.exp(-a)))) * b
        y_ref[...] = jnp.dot(h.astype(BF16), wd_buf[slot], preferred_element_type=F32).astype(y_ref.dtype)

    @pl.when(i == n)
    def _():
        y_ref[...] = jnp.zeros_like(y_ref)


def _expert_ffn(xe, cap, wg, wu, wd, layer):
    E, _, D = xe.shape
    F = wg.shape[3]
    tm = min(ROW_TILE, cap)
    n = cap // tm
    any_spec = pl.BlockSpec(memory_space=pl.ANY)
    return pl.pallas_call(
        functools.partial(_ffn_kernel, layer=layer),
        grid=(E, n + 1),
        in_specs=[pl.BlockSpec((None, tm, D), lambda e, i: (e, jnp.minimum(i, n - 1), 0)),
                  any_spec, any_spec, any_spec],
        out_specs=pl.BlockSpec((None, tm, D), lambda e, i: (e, i, 0)),
        out_shape=jax.ShapeDtypeStruct((E, cap + tm, D), BF16),
        scratch_shapes=[pltpu.VMEM((2, D, F), BF16), pltpu.VMEM((2, D, F), BF16), pltpu.VMEM((2, F, D), BF16),
                        pltpu.VMEM((D // n, F), F32), pltpu.VMEM((D // n, F), F32), pltpu.VMEM((F // n, D), F32),
                        pltpu.SemaphoreType.DMA((3,))],
        compiler_params=_params(2),
        name=f"expert_ffn_cap{cap}",
    )(xe, wg, wu, wd)


def _moe(x, hn, aff, wg, wu, wd, layer, final_g):
    T, D = x.shape
    cap = max(1, (EC_CAPACITY * T) // N_EXPERTS)
    sel, cnt, pos = _select(aff, cap)
    xe = _dispatch(sel, cnt, pos, hn, cap)
    y = _expert_ffn(xe, cap, wg, wu, wd, layer)
    return _combine(sel, aff, cnt, pos, x, y, final_g)


def _trunk(x3, p):
    B, S, D = x3.shape
    x = x3.reshape(B * S, D)
    tables = _rope_tables(S)
    depth = p["ffn_norm"].shape[0]
    for i in range(depth):
        j = i // 2
        g_ffn, w_rt = p["ffn_norm"][i][None], p["w_router_t"][i]
        if i % 2 == 0:
            qkvs = _qkv_proj(x, p["attn_norm"][j][None], p["w_qkv"][j], tables, B, S)
            res = [_dilated_attention(qkvs[g], g) for g in range(N_GROUPS)]
            x, hn, aff = _attn_out_proj([r[0] for r in res], [r[1] for r in res], p["w_attn_out"][j], x,
                                        g_ffn, w_rt, S)
        else:
            x, hn, aff = _short_conv(x, p["conv_norm"][j][None], p["w_conv_in"][j], p["conv_w"][j],
                                     p["w_conv_out"][j], g_ffn, w_rt, S)
        final_g = p["final_norm"][None] if i == depth - 1 else None
        x = _moe(x, hn, aff, p["w_gate"], p["w_up"], p["w_down"], i, final_g)
    return x.reshape(B, S, D)


def kernel(x_prompt, x_sample, attn_norm, w_qkv, w_attn_out, conv_norm, w_conv_in, conv_w, w_conv_out,
           ffn_norm, w_router, w_gate, w_up, w_down, final_norm):
    per_layer = lambda w: [w[i].astype(BF16) for i in range(w.shape[0])]
    p = dict(
        attn_norm=attn_norm, conv_norm=conv_norm, ffn_norm=ffn_norm, final_norm=final_norm, conv_w=conv_w,
        w_qkv=per_layer(w_qkv), w_attn_out=per_layer(w_attn_out),
        w_conv_in=per_layer(w_conv_in), w_conv_out=per_layer(w_conv_out),
        w_router_t=per_layer(jnp.swapaxes(w_router, 1, 2)),
        w_gate=w_gate, w_up=w_up, w_down=w_down,
    )
    return (_trunk(x_prompt, p), _trunk(x_sample, p))
```

```python
import functools

import jax
import jax.numpy as jnp
from jax import lax
from jax.experimental import pallas as pl
from jax.experimental.pallas import tpu as pltpu

D_MODEL = 1024
HEAD_DIM = 64
HEADS_PER_GROUP = 4
DILATED_GROUPS = ((128, 1), (512, 4), (2048, 16))
N_GROUPS = len(DILATED_GROUPS)
ATTN_WIDTH = N_GROUPS * HEADS_PER_GROUP * HEAD_DIM
GROUP_WIDTH = HEADS_PER_GROUP * HEAD_DIM
ROPE_DIM = HEAD_DIM // 4
ROPE_THETA = 500000.0
N_EXPERTS = 16
EC_CAPACITY = 2
RMS_EPS = 1e-6
NEG_INF = -1e30

LANES = 128
SUB = 8
SUB_BF16 = 16
ROW_TILE = 512
ATTN_TQ = 128
MOE_TILE = 256
CHUNK = 256
PIECE = 16
WIDE = 64
LOG2_PIECE = 4
LOG2_CHUNK = 8
LOG2_SUB = 3
LOG2_WIDE_PIECES = 2
assert (1 << LOG2_PIECE, 1 << LOG2_CHUNK, 1 << LOG2_SUB) == (PIECE, CHUNK, SUB)
assert PIECE << LOG2_WIDE_PIECES == WIDE
VMEM_LIMIT = 56 * 1024 * 1024

F32 = jnp.float32
BF16 = jnp.bfloat16


def _params(n_axes):
    return pltpu.CompilerParams(
        dimension_semantics=("arbitrary",) * n_axes, vmem_limit_bytes=VMEM_LIMIT)


def _rms(x, g):
    ms = jnp.mean(x * x, axis=-1, keepdims=True)
    return x * lax.rsqrt(ms + RMS_EPS) * g


def _qkv_kernel(x_ref, g_ref, w_ref, cos_ref, sa_ref, sb_ref, o0_ref, o1_ref, o2_ref, s_ref):
    tm = x_ref.shape[0]
    h = _rms(x_ref[...], g_ref[...]).astype(BF16)
    acc = jnp.dot(h, w_ref[...], preferred_element_type=F32)
    c, sa, sb = cos_ref[...], sa_ref[...], sb_ref[...]
    n_q = ATTN_WIDTH // LANES
    for j in range(2 * n_q):
        blk = acc[:, j * LANES:(j + 1) * LANES]
        r = blk * c + pltpu.roll(blk, LANES - ROPE_DIM // 2, 1) * sa + pltpu.roll(blk, ROPE_DIM // 2, 1) * sb
        if j < n_q:
            r = r * (HEAD_DIM ** -0.5)
        s_ref[j] = r
    for j in range(2 * n_q, 3 * n_q):
        s_ref[j] = acc[:, j * LANES:(j + 1) * LANES]
    per_group = GROUP_WIDTH // LANES
    for g, o_ref in enumerate((o0_ref, o1_ref, o2_ref)):
        dil = DILATED_GROUPS[g][1]
        for part in range(3):
            for jj in range(per_group):
                src = part * n_q + g * per_group + jj
                dst = slice((part * per_group + jj) * LANES, (part * per_group + jj + 1) * LANES)
                for r in range(dil):
                    rows = pl.ds(r, tm // dil, stride=dil) if dil > 1 else slice(None)
                    o_ref[r, :, dst] = s_ref[src, rows, :].astype(BF16)


def _rope_tables(S):
    half = ROPE_DIM // 2
    inv_freq = jnp.power(ROPE_THETA, -jnp.arange(0, ROPE_DIM, 2, dtype=F32) / ROPE_DIM)
    ang = jnp.arange(S, dtype=F32)[:, None] * inv_freq[None, :]
    cos, sin = jnp.cos(ang), jnp.sin(ang)
    rest = HEAD_DIM - ROPE_DIM
    c = jnp.concatenate([cos, cos, jnp.ones((S, rest), F32)], axis=1)
    sa = jnp.concatenate([-sin, jnp.zeros((S, half + rest), F32)], axis=1)
    sb = jnp.concatenate([jnp.zeros((S, half), F32), sin, jnp.zeros((S, rest), F32)], axis=1)
    rep = LANES // HEAD_DIM
    return tuple(jnp.tile(t, (1, rep)) for t in (c, sa, sb))


def _qkv_proj(x, g, w, tables, B, S):
    T, D = x.shape
    N = w.shape[1]
    tm = ROW_TILE
    per_seq = S // tm
    tab_spec = pl.BlockSpec((tm, LANES), lambda i: (i % per_seq, 0))
    dils = [d for _, d in DILATED_GROUPS]
    return pl.pallas_call(
        _qkv_kernel,
        grid=(T // tm,),
        in_specs=[
            pl.BlockSpec((tm, D), lambda i: (i, 0)),
            pl.BlockSpec((1, D), lambda i: (0, 0)),
            pl.BlockSpec((D, N), lambda i: (0, 0)),
            tab_spec, tab_spec, tab_spec,
        ],
        out_specs=[pl.BlockSpec((None, d, tm // d, 3 * GROUP_WIDTH), lambda i: (i // per_seq, 0, i % per_seq, 0))
                   for d in dils],
        out_shape=[jax.ShapeDtypeStruct((B, d, S // d, 3 * GROUP_WIDTH), BF16) for d in dils],
        scratch_shapes=[pltpu.VMEM((N // LANES, tm, LANES), F32)],
        compiler_params=_params(1),
        name="qkv_proj",
    )(x, g, w, *tables)


def _attn_kernel(q_ref, k_ref, v_ref, o_ref, l_ref, *, L, Lq, kw, tq, radius):
    c = pl.program_id(2)

    def body(i, carry):
        r0 = pl.multiple_of(i * tq, tq)
        t0 = c * Lq + i * tq
        ks = pl.multiple_of(jnp.clip(t0 - radius, 0, L - kw), radius)
        qb = q_ref[pl.ds(r0, tq), :]
        kb = k_ref[pl.ds(ks, kw), :]
        vb = v_ref[pl.ds(ks, kw), :]
        qi = t0 + lax.broadcasted_iota(jnp.int32, (tq, kw), 0)
        kj = ks + lax.broadcasted_iota(jnp.int32, (tq, kw), 1)
        valid = jnp.abs(qi - kj) <= radius
        per_block = LANES // HEAD_DIM
        lane_head = lax.broadcasted_iota(jnp.int32, (1, LANES), 1) // HEAD_DIM
        for blk in range(GROUP_WIDTH // LANES):
            lanes = slice(blk * LANES, (blk + 1) * LANES)
            qp, kp, vp = qb[:, lanes], kb[:, lanes], vb[:, lanes]
            o_blk = jnp.zeros((tq, LANES), F32)
            lse_blk = jnp.zeros((tq, LANES), F32)
            for hh in range(per_block):
                mine = lane_head == hh
                s = lax.dot_general(jnp.where(mine, qp, jnp.zeros_like(qp)), kp, (((1,), (1,)), ((), ())),
                                    preferred_element_type=F32)
                s = jnp.where(valid, s, NEG_INF)
                m = jnp.max(s, axis=-1, keepdims=True)
                p = jnp.exp(s - m)
                l = jnp.sum(p, axis=-1, keepdims=True)
                o_blk = o_blk + jnp.dot(p.astype(BF16), jnp.where(mine, vp, jnp.zeros_like(vp)),
                                        preferred_element_type=F32) / l
                lse_blk = jnp.where(mine, m + jnp.log(l), lse_blk)
            o_ref[pl.ds(r0, tq), lanes] = o_blk
            l_ref[pl.ds(r0, tq), lanes] = lse_blk
        return carry

    n_blk = Lq // tq
    lax.fori_loop(0, n_blk, body, 0, unroll=2 if n_blk % 2 == 0 else 1)


def _dilated_attention(qkv, group):
    window, dil = DILATED_GROUPS[group]
    B, _, L, _ = qkv.shape
    radius = window // (2 * dil)
    tq = min(ATTN_TQ, L)
    kw = min(L, tq + 2 * radius)
    Lq = min(L, 1024)
    assert L % tq == 0 and L % Lq == 0 and Lq % tq == 0 and radius % 16 == 0 and (L - kw) % radius == 0
    kern = functools.partial(_attn_kernel, L=L, Lq=Lq, kw=kw, tq=tq, radius=radius)
    out_spec = pl.BlockSpec((None, None, Lq, GROUP_WIDTH), lambda b, r, c: (b, r, c, 0))
    return pl.pallas_call(
        kern,
        grid=(B, dil, L // Lq),
        in_specs=[
            pl.BlockSpec((None, None, Lq, GROUP_WIDTH), lambda b, r, c: (b, r, c, 0)),
            pl.BlockSpec((None, None, L, GROUP_WIDTH), lambda b, r, c: (b, r, 0, 1)),
            pl.BlockSpec((None, None, L, GROUP_WIDTH), lambda b, r, c: (b, r, 0, 2)),
        ],
        out_specs=[out_spec, out_spec],
        out_shape=[jax.ShapeDtypeStruct((B, dil, L, GROUP_WIDTH), F32)] * 2,
        compiler_params=_params(3),
        name=f"dilated_attn_g{group}",
    )(qkv, qkv, qkv)


def _oproj_kernel(o0_ref, o1_ref, o2_ref, l0_ref, l1_ref, l2_ref, w_ref, x_ref, gf_ref, wr_ref,
                  y_ref, hn_ref, aff_ref, o_scr, l_scr):
    tm = x_ref.shape[0]
    for g, (o_ref, l_ref) in enumerate(((o0_ref, l0_ref), (o1_ref, l1_ref), (o2_ref, l2_ref))):
        dil = DILATED_GROUPS[g][1]
        for r in range(dil):
            rows = pl.ds(r, tm // dil, stride=dil) if dil > 1 else slice(None)
            for jj in range(GROUP_WIDTH // LANES):
                lanes = slice(jj * LANES, (jj + 1) * LANES)
                o_scr[g, jj, rows, :] = o_ref[r, :, lanes]
                l_scr[g, jj, rows, :] = l_ref[r, :, lanes]
    zs = []
    for jj in range(GROUP_WIDTH // LANES):
        l0, l1, l2 = l_scr[0, jj], l_scr[1, jj], l_scr[2, jj]
        m = jnp.maximum(jnp.maximum(l0, l1), l2)
        e0, e1, e2 = jnp.exp(l0 - m), jnp.exp(l1 - m), jnp.exp(l2 - m)
        den = e0 + e1 + e2
        zs.append((o_scr[0, jj] * (e0 / den), o_scr[1, jj] * (e1 / den), o_scr[2, jj] * (e2 / den)))
    z = jnp.concatenate([zs[jj][g] for g in range(N_GROUPS) for jj in range(GROUP_WIDTH // LANES)], axis=1)
    y = x_ref[...] + jnp.dot(z.astype(BF16), w_ref[...], preferred_element_type=F32)
    y_ref[...] = y
    _route(y, gf_ref, wr_ref, hn_ref, aff_ref)


def _attn_out_proj(outs, lses, w, x, g_ffn, w_rt, S):
    T, D = x.shape
    tm = ROW_TILE
    per_seq = S // tm
    gspecs = [pl.BlockSpec((None, d, tm // d, GROUP_WIDTH), lambda i: (i // per_seq, 0, i % per_seq, 0))
              for _, d in DILATED_GROUPS]
    xspec = pl.BlockSpec((tm, D), lambda i: (i, 0))
    r_in, r_out, r_shapes = _route_specs(T, D, w_rt.shape[0], tm)
    return pl.pallas_call(
        _oproj_kernel,
        grid=(T // tm,),
        in_specs=gspecs + gspecs + [pl.BlockSpec((ATTN_WIDTH, D), lambda i: (0, 0)), xspec] + r_in,
        out_specs=[xspec] + r_out,
        out_shape=[jax.ShapeDtypeStruct((T, D), F32)] + r_shapes,
        scratch_shapes=[pltpu.VMEM((N_GROUPS, GROUP_WIDTH // LANES, tm, LANES), F32)] * 2,
        compiler_params=_params(1),
        name="attn_out_proj",
    )(*outs, *lses, w, x, g_ffn, w_rt)


def _route(x, gf_ref, wr_ref, hn_ref, aff_ref):
    h = _rms(x, gf_ref[...]).astype(BF16)
    hn_ref[...] = h
    logits = lax.dot_general(wr_ref[...], h, (((1,), (1,)), ((), ())),
                             preferred_element_type=F32)
    m = jnp.max(logits, axis=0, keepdims=True)
    e = jnp.exp(logits - m)
    aff = e / jnp.sum(e, axis=0, keepdims=True)
    for k in range(aff_ref.shape[0]):
        aff_ref[k] = aff[:, k * MOE_TILE:(k + 1) * MOE_TILE]


def _route_specs(T, D, E, tm):
    per = tm // MOE_TILE
    in_specs = [pl.BlockSpec((1, D), lambda i: (0, 0)), pl.BlockSpec((E, D), lambda i: (0, 0))]
    out_specs = [pl.BlockSpec((tm, D), lambda i: (i, 0)), pl.BlockSpec((per, E, MOE_TILE), lambda i: (i, 0, 0))]
    out_shapes = [jax.ShapeDtypeStruct((T, D), BF16), jax.ShapeDtypeStruct((T // MOE_TILE, E, MOE_TILE), F32)]
    return in_specs, out_specs, out_shapes


def _conv_kernel(x_ref, xp_ref, xn_ref, g_ref, win_ref, cw_ref, wout_ref, gf_ref, wr_ref,
                 y_ref, hn_ref, aff_ref, u_scr, *, tm, per_seq):
    i = pl.program_id(0)
    D = x_ref.shape[1]
    x = x_ref[...]
    xa = jnp.concatenate([xp_ref[...], x, xn_ref[...]], axis=0)
    h = _rms(xa, g_ref[...]).astype(BF16)
    full = jnp.dot(h, win_ref[...], preferred_element_type=F32)
    u_scr[...] = full[:, D:2 * D] * full[:, 2 * D:]
    first = (i % per_seq) == 0
    last = (i % per_seq) == per_seq - 1
    u_scr[pl.ds(7, 1), :] = jnp.where(first, 0.0, u_scr[pl.ds(7, 1), :])
    u_scr[pl.ds(tm + 8, 1), :] = jnp.where(last, 0.0, u_scr[pl.ds(tm + 8, 1), :])
    cw = cw_ref[...]
    conv = (cw[0:1] * u_scr[pl.ds(7, tm), :] + cw[1:2] * u_scr[pl.ds(8, tm), :]
            + cw[2:3] * u_scr[pl.ds(9, tm), :])
    z = (full[8:8 + tm, :D] * conv).astype(BF16)
    y = x + jnp.dot(z, wout_ref[...], preferred_element_type=F32)
    y_ref[...] = y
    _route(y, gf_ref, wr_ref, hn_ref, aff_ref)


def _short_conv(x, g, w_in, cw, w_out, g_ffn, w_rt, S):
    T, D = x.shape
    tm = ROW_TILE
    per_seq = S // tm
    sub = tm // 8
    n8 = T // 8
    kern = functools.partial(_conv_kernel, tm=tm, per_seq=per_seq)
    r_in, r_out, r_shapes = _route_specs(T, D, w_rt.shape[0], tm)
    return pl.pallas_call(
        kern,
        grid=(T // tm,),
        in_specs=[
            pl.BlockSpec((tm, D), lambda i: (i, 0)),
            pl.BlockSpec((8, D), lambda i: (jnp.maximum(i * sub - 1, 0), 0)),
            pl.BlockSpec((8, D), lambda i: (jnp.minimum((i + 1) * sub, n8 - 1), 0)),
            pl.BlockSpec((1, D), lambda i: (0, 0)),
            pl.BlockSpec((D, 3 * D), lambda i: (0, 0)),
            pl.BlockSpec((3, D), lambda i: (0, 0)),
            pl.BlockSpec((D, D), lambda i: (0, 0)),
        ] + r_in,
        out_specs=[pl.BlockSpec((tm, D), lambda i: (i, 0))] + r_out,
        out_shape=[jax.ShapeDtypeStruct((T, D), F32)] + r_shapes,
        scratch_shapes=[pltpu.VMEM((tm + 16, D), F32)],
        compiler_params=_params(1),
        name="short_conv",
    )(x, x, x, g, w_in, cw, w_out, g_ffn, w_rt)


def _strict_upper(n):
    r = lax.broadcasted_iota(jnp.int32, (n, n), 0)
    c = lax.broadcasted_iota(jnp.int32, (n, n), 1)
    return (r < c).astype(BF16)


def _select_kernel(aff_ref, sel_ref, cnt_ref, pos_ref, *, cap):
    NT, E, tm = aff_ref.shape
    capf = float(cap)

    def count_ge(v):
        bits = lax.bitcast_convert_type(aff_ref[...], jnp.int32)
        ge = (bits >= v[None]).astype(F32)
        return jnp.sum(jnp.sum(ge, axis=0), axis=1, keepdims=True)

    def bisect(_, lh):
        lo, hi = lh
        mid = lo + lax.shift_right_logical(hi - lo, 1)
        ok = count_ge(mid) >= capf
        return jnp.where(ok, mid, lo), jnp.where(ok, hi, mid)

    one_bits = 0x3F800000
    lo0 = jnp.zeros((E, 1), jnp.int32)
    hi0 = jnp.full((E, 1), one_bits + 1, jnp.int32)
    tau, _ = lax.fori_loop(0, 31, bisect, (lo0, hi0))
    need = capf - count_ge(tau + 1)
    tri = _strict_upper(tm)

    def tile(j, carry):
        ceq, csel = carry
        bits = lax.bitcast_convert_type(aff_ref[j], jnp.int32)
        gt = bits > tau
        eq = (bits == tau).astype(F32)
        rank_eq = jnp.dot(eq.astype(BF16), tri, preferred_element_type=F32) + ceq
        sel = jnp.where(gt | ((eq > 0) & (rank_eq < need)), 1.0, 0.0)
        sel_ref[j] = sel
        c = jnp.sum(sel, axis=1, keepdims=True)
        cnt_ref[j] = jnp.broadcast_to(c, (E, LANES))
        pos_ref[j] = jnp.broadcast_to(csel, (E, LANES))
        return ceq + jnp.sum(eq, axis=1, keepdims=True), csel + c

    zero = jnp.zeros((E, 1), F32)
    lax.fori_loop(0, NT, tile, (zero, zero))


def _select(aff, cap):
    NT, E, tm = aff.shape
    sel, cnt, pos = pl.pallas_call(
        functools.partial(_select_kernel, cap=cap),
        out_shape=[jax.ShapeDtypeStruct((NT, E, tm), F32),
                   jax.ShapeDtypeStruct((NT, E, LANES), F32),
                   jax.ShapeDtypeStruct((NT, E, LANES), F32)],
        compiler_params=pltpu.CompilerParams(vmem_limit_bytes=VMEM_LIMIT),
        name="expert_select",
    )(aff)
    to_smem = lambda a: a[:, :, 0].astype(jnp.int32).reshape(NT * E)
    return sel, to_smem(cnt), to_smem(pos)


def _stack_rows(E, tm, sub):
    worst = E * tm + E * (sub - 1 + PIECE - 1)
    return -(-worst // CHUNK) * CHUNK


def _cdiv_pow2(x, log2):
    return lax.shift_right_logical(x + ((1 << log2) - 1), log2)


def _segments(cnt_ref, pos_ref, tile, E, sub):
    segs = []
    base = jnp.int32(0)
    for e in range(E):
        n = cnt_ref[tile * E + e]
        p0 = pos_ref[tile * E + e]
        a = p0 & (sub - 1)
        pieces = jnp.where(n > 0, _cdiv_pow2(a + n, LOG2_PIECE), 0)
        segs.append((base, pieces, p0 - a, a, n))
        base = base + pieces * PIECE
    return segs, base


NO_RANK = -256.0
NO_OWNER = 1024.0


def _segment_rows(segs):
    lane = lax.broadcasted_iota(jnp.int32, (1, LANES), 1)
    first = jnp.full((1, LANES), float(NO_OWNER * NO_OWNER), F32)
    end = jnp.full((1, LANES), -1.0, F32)
    zero = jnp.zeros((1, LANES), F32)
    for e, (base, pieces, _, a, _) in enumerate(segs):
        m = lane == e
        first = jnp.where(m, base.astype(F32), first)
        end = jnp.where(m, (base + pieces * PIECE).astype(F32), end)
        zero = jnp.where(m, (base + a).astype(F32), zero)
    return first, end, zero


def _tile_ranks(sel, tm):
    E = sel.shape[0]
    rank = jnp.dot(sel.astype(BF16), _strict_upper(tm), preferred_element_type=F32)
    rank = jnp.where(sel > 0, rank, NO_RANK).astype(BF16)
    return jnp.concatenate([rank, jnp.zeros((LANES - E, tm), BF16)], axis=0)


def _chunk_one_hot(j0, rows, ranks, E):
    first, end, zero = rows
    jv = (j0 + lax.broadcasted_iota(jnp.int32, (CHUNK, LANES), 0)).astype(F32)
    own = (jv >= first) & (jv < end)
    lane = lax.broadcasted_iota(jnp.int32, (CHUNK, LANES), 1)
    tgt = jnp.sum(jnp.where(own & (lane < E), jv - zero + NO_OWNER, 0.0), axis=1, keepdims=True) - NO_OWNER
    ownb = jnp.where(own, 1.0, 0.0).astype(BF16)
    hit = jnp.dot(ownb, ranks, preferred_element_type=F32) == tgt
    return ownb, hit


def _total_pieces(segs):
    t = segs[0][1]
    for s in segs[1:]:
        t = t + s[1]
    return t


WAIT_GROUP = 8
LOG2_WAIT_GROUP = 3
assert 1 << LOG2_WAIT_GROUP == WAIT_GROUP


def _wait_pieces(n, copy_of_rows, unit=PIECE):
    def waits(count, rows):
        def w(_, c):
            copy_of_rows(rows).wait()
            return c
        lax.fori_loop(0, count, w, 0)
    waits(lax.shift_right_logical(n, LOG2_WAIT_GROUP), WAIT_GROUP * unit)
    waits(n & (WAIT_GROUP - 1), unit)


def _dispatch_kernel(cnt_ref, pos_ref, sel_ref, hn_ref, xe_ref, stack, pend, prev_copies, sem):
    i = pl.program_id(0)
    nt = pl.num_programs(0)
    slot = i % 2
    E, tm = sel_ref.shape

    @pl.when(i == 0)
    def _():
        pend[...] = jnp.zeros_like(pend)
        stack[...] = jnp.zeros_like(stack)

    segs, total = _segments(cnt_ref, pos_ref, i, E, SUB)
    rows = _segment_rows(segs)
    ranks = _tile_ranks(sel_ref[...], tm)

    def chunk(k, c):
        j0 = pl.multiple_of(k * CHUNK, CHUNK)
        _, hit = _chunk_one_hot(j0, rows, ranks, E)
        oh = jnp.where(hit, 1.0, 0.0).astype(BF16)
        stack[slot, pl.ds(j0, CHUNK), :] = jnp.dot(oh, hn_ref[...], preferred_element_type=F32)
        return c

    lax.fori_loop(0, _cdiv_pow2(total, LOG2_CHUNK), chunk, 0)

    spare = stack.shape[1] - PIECE
    for e in range(E):
        base, _, _, a, n = segs[e]
        live = n > 0
        first = pl.multiple_of(jnp.where(live, base, spare), SUB)
        last = pl.multiple_of(
            jnp.where(live, base + lax.shift_right_logical(a + n - 1, LOG2_SUB) * SUB, spare), SUB)
        stack[slot, pl.ds(first, SUB), :] = stack[slot, pl.ds(first, SUB), :] + jnp.where(live, pend[e], 0.0)
        incomplete = live & (((a + n) & (SUB - 1)) != 0)
        pend[e] = jnp.where(live, jnp.where(incomplete, stack[slot, pl.ds(last, SUB), :], 0.0), pend[e])

    def wide_copy(s, src_row, e, dst_row, rows=WIDE):
        return pltpu.make_async_copy(stack.at[s, pl.ds(src_row, rows), :],
                                     xe_ref.at[e, pl.ds(dst_row, rows), :], sem.at[s])

    def copies_of(some_segs):
        t = jnp.int32(0)
        for _, pieces, _, _, _ in some_segs:
            t = t + _cdiv_pow2(pieces, LOG2_WIDE_PIECES)
        return t

    waiter = lambda s: (lambda rows: wide_copy(s, 0, 0, 0, rows))

    n_copies = copies_of(segs)

    @pl.when(i > 0)
    def _():
        _wait_pieces(prev_copies[0], waiter(1 - slot), WIDE)

    prev_copies[0] = n_copies

    for e in range(E):
        base, pieces, start, _, _ = segs[e]

        def issue(c, carry, base=base, start=start, e=e):
            wide_copy(slot, pl.multiple_of(base + c * WIDE, PIECE), e,
                      pl.multiple_of(start + c * WIDE, SUB)).start()
            return carry

        lax.fori_loop(0, _cdiv_pow2(pieces, LOG2_WIDE_PIECES), issue, 0)

    @pl.when(i == nt - 1)
    def _():
        _wait_pieces(n_copies, waiter(slot), WIDE)
        cap = xe_ref.shape[1] - WIDE
        stack[slot, pl.ds(0, WIDE), :] = jnp.zeros((WIDE, stack.shape[2]), F32)
        for e in range(E):
            wide_copy(slot, 0, e, cap).start()
        _wait_pieces(E, waiter(slot), WIDE)


def _dispatch(sel, cnt, pos, hn, cap):
    NT, E, tm = sel.shape
    T, D = hn.shape
    return pl.pallas_call(
        _dispatch_kernel,
        grid_spec=pltpu.PrefetchScalarGridSpec(
            num_scalar_prefetch=2,
            grid=(NT,),
            in_specs=[pl.BlockSpec((None, E, tm), lambda i, c, p: (i, 0, 0)),
                      pl.BlockSpec((tm, D), lambda i, c, p: (i, 0))],
            out_specs=pl.BlockSpec(memory_space=pl.ANY),
            scratch_shapes=[pltpu.VMEM((2, _stack_rows(E, tm, SUB) + WIDE, D), F32),
                            pltpu.VMEM((E, SUB, D), F32),
                            pltpu.SMEM((1,), jnp.int32),
                            pltpu.SemaphoreType.DMA((2,))]),
        out_shape=jax.ShapeDtypeStruct((E, cap + WIDE, D), F32),
        compiler_params=_params(1),
        name="expert_dispatch",
    )(cnt, pos, sel, hn)


def _segment_cols(segs):
    E = len(segs)
    eidx = lax.broadcasted_iota(jnp.int32, (E, 1), 0)
    first = jnp.zeros((E, 1), F32)
    end = jnp.zeros((E, 1), F32)
    zero = jnp.zeros((E, 1), F32)
    for e, (base, pieces, _, a, _) in enumerate(segs):
        m = eidx == e
        first = jnp.where(m, base.astype(F32), first)
        end = jnp.where(m, (base + pieces * PIECE).astype(F32), end)
        zero = jnp.where(m, (base + a).astype(F32), zero)
    return first, end, zero


def _combine_kernel(cnt_ref, pos_ref, sel_ref, aff_ref, x_ref, y_hbm, *rest):
    g_ref = rest[0] if len(rest) == 4 else None
    o_ref, stack, sem = rest[-3:]
    i = pl.program_id(0)
    nt = pl.num_programs(0)
    slot = i % 2
    E, tm = sel_ref.shape

    def piece_copy(s, e, src_row, dst_row, rows=PIECE):
        return pltpu.make_async_copy(y_hbm.at[e, pl.ds(src_row, rows), :],
                                     stack.at[s, pl.ds(dst_row, rows), :], sem.at[s])

    def fetch(tile, s):
        segs, _ = _segments(cnt_ref, pos_ref, tile, E, SUB_BF16)
        for e in range(E):
            base, pieces, start, _, _ = segs[e]

            def issue(c, carry, base=base, start=start, e=e):
                piece_copy(s, e, pl.multiple_of(start + c * PIECE, SUB_BF16),
                           pl.multiple_of(base + c * PIECE, PIECE)).start()
                return carry

            lax.fori_loop(0, pieces, issue, 0)

    @pl.when(i == 0)
    def _():
        stack[...] = jnp.zeros_like(stack)
        fetch(0, 0)

    @pl.when(i + 1 < nt)
    def _():
        fetch(i + 1, 1 - slot)

    segs, total = _segments(cnt_ref, pos_ref, i, E, SUB_BF16)
    first, end, zero = _segment_cols(segs)
    sel = sel_ref[...]
    rank = jnp.dot(sel.astype(BF16), _strict_upper(tm), preferred_element_type=F32)
    rank = jnp.where(sel > 0, rank, NO_RANK)
    pad = jnp.zeros((LANES - 2 * E, tm), F32)
    by_token = jnp.transpose(jnp.concatenate([aff_ref[...], rank, pad], axis=0)).astype(BF16)
    o_ref[...] = x_ref[...]
    _wait_pieces(_total_pieces(segs), lambda rows: piece_copy(slot, 0, 0, 0, rows))

    def chunk(k, c):
        j0 = pl.multiple_of(k * CHUNK, CHUNK)
        jl = (j0 + lax.broadcasted_iota(jnp.int32, (E, CHUNK), 1)).astype(F32)
        own = (jl >= first) & (jl < end)
        tgt = jnp.sum(jnp.where(own, jl - zero + NO_OWNER, 0.0), axis=0, keepdims=True) - NO_OWNER
        o16 = jnp.where(own, 1.0, 0.0).astype(BF16)
        zeros = lambda n: jnp.zeros((n, CHUNK), BF16)
        own_aff = jnp.concatenate([o16, zeros(LANES - E)], axis=0)
        own_rank = jnp.concatenate([zeros(E), o16, zeros(LANES - 2 * E)], axis=0)
        hit = jnp.dot(by_token, own_rank, preferred_element_type=F32) == tgt
        g = jnp.where(hit, jnp.dot(by_token, own_aff, preferred_element_type=F32), 0.0).astype(BF16)
        o_ref[...] += jnp.dot(g, stack[slot, pl.ds(j0, CHUNK), :], preferred_element_type=F32)
        return c

    lax.fori_loop(0, _cdiv_pow2(total, LOG2_CHUNK), chunk, 0)
    if g_ref is not None:
        o_ref[...] = _rms(o_ref[...], g_ref[...])


def _combine(sel, aff, cnt, pos, x, y, final_g=None):
    NT, E, tm = sel.shape
    T, D = x.shape
    tile_spec = pl.BlockSpec((None, E, tm), lambda i, c, p: (i, 0, 0))
    x_spec = pl.BlockSpec((tm, D), lambda i, c, p: (i, 0))
    extra_specs = [] if final_g is None else [pl.BlockSpec((1, D), lambda i, c, p: (0, 0))]
    extra = [] if final_g is None else [final_g]
    return pl.pallas_call(
        _combine_kernel,
        grid_spec=pltpu.PrefetchScalarGridSpec(
            num_scalar_prefetch=2,
            grid=(NT,),
            in_specs=[tile_spec, tile_spec, x_spec, pl.BlockSpec(memory_space=pl.ANY)] + extra_specs,
            out_specs=x_spec,
            scratch_shapes=[pltpu.VMEM((2, _stack_rows(E, tm, SUB_BF16), D), BF16),
                            pltpu.SemaphoreType.DMA((2,))]),
        out_shape=jax.ShapeDtypeStruct((T, D), F32),
        compiler_params=_params(1),
        name="expert_combine" if final_g is None else "expert_combine_norm",
    )(cnt, pos, sel, aff, x, y, *extra)


def _ffn_kernel(x_ref, wg_hbm, wu_hbm, wd_hbm, y_ref, wg_buf, wu_buf, wd_buf, wg_stg, wu_stg, wd_stg, sem,
                *, layer):
    e = pl.program_id(0)
    i = pl.program_id(1)
    n_e = pl.num_programs(0)
    n = pl.num_programs(1) - 1
    slot = e % 2
    stages = ((wg_hbm, wg_stg, wg_buf), (wu_hbm, wu_stg, wu_buf), (wd_hbm, wd_stg, wd_buf))

    def chunk_copy(k, expert, c):
        hbm, stg, _ = stages[k]
        rows = stg.shape[0]
        return pltpu.make_async_copy(hbm.at[layer, expert, pl.ds(pl.multiple_of(c * rows, rows), rows), :],
                                     stg, sem.at[k])

    def land(expert, c, dst_slot):
        for k, (_, stg, buf) in enumerate(stages):
            rows = stg.shape[0]
            chunk_copy(k, expert, c).wait()
            buf[dst_slot, pl.ds(pl.multiple_of(c * rows, rows), rows), :] = stg[...].astype(BF16)

    @pl.when((e == 0) & (i == 0))
    def _():
        def first(c, carry):
            for k in range(3):
                chunk_copy(k, 0, c).start()
            land(0, c, 0)
            return carry
        lax.fori_loop(0, n, first, 0)

    nxt = jnp.minimum(e + 1, n_e - 1)

    @pl.when((e + 1 < n_e) & (i > 0))
    def _():
        land(nxt, i - 1, 1 - slot)

    @pl.when((e + 1 < n_e) & (i < n))
    def _():
        for k in range(3):
            chunk_copy(k, nxt, i).start()

    @pl.when(i < n)
    def _():
        x = x_ref[...].astype(BF16)
        a = jnp.dot(x, wg_buf[slot], preferred_element_type=F32)
        b = jnp.dot(x, wu_buf[slot], preferred_element_type=F32)
        h = (a * (1.0 / (1.0 + jnp.exp(-a)))) * b
        y_ref[...] = jnp.dot(h.astype(BF16), wd_buf[slot], preferred_element_type=F32).astype(y_ref.dtype)

    @pl.when(i == n)
    def _():
        y_ref[...] = jnp.zeros_like(y_ref)


def _expert_ffn(xe, cap, wg, wu, wd, layer):
    E, _, D = xe.shape
    F = wg.shape[3]
    tm = min(ROW_TILE, cap)
    n = cap // tm
    any_spec = pl.BlockSpec(memory_space=pl.ANY)
    return pl.pallas_call(
        functools.partial(_ffn_kernel, layer=layer),
        grid=(E, n + 1),
        in_specs=[pl.BlockSpec((None, tm, D), lambda e, i: (e, jnp.minimum(i, n - 1), 0)),
                  any_spec, any_spec, any_spec],
        out_specs=pl.BlockSpec((None, tm, D), lambda e, i: (e, i, 0)),
        out_shape=jax.ShapeDtypeStruct((E, cap + tm, D), BF16),
        scratch_shapes=[pltpu.VMEM((2, D, F), BF16), pltpu.VMEM((2, D, F), BF16), pltpu.VMEM((2, F, D), BF16),
                        pltpu.VMEM((D // n, F), F32), pltpu.VMEM((D // n, F), F32), pltpu.VMEM((F // n, D), F32),
                        pltpu.SemaphoreType.DMA((3,))],
        compiler_params=_params(2),
        name=f"expert_ffn_cap{cap}",
    )(xe, wg, wu, wd)


def _moe(x, hn, aff, wg, wu, wd, layer, final_g):
    T, D = x.shape
    cap = max(1, (EC_CAPACITY * T) // N_EXPERTS)
    sel, cnt, pos = _select(aff, cap)
    xe = _dispatch(sel, cnt, pos, hn, cap)
    y = _expert_ffn(xe, cap, wg, wu, wd, layer)
    return _combine(sel, aff, cnt, pos, x, y, final_g)


def _trunk(x3, p):
    B, S, D = x3.shape
    x = x3.reshape(B * S, D)
    tables = _rope_tables(S)
    depth = p["ffn_norm"].shape[0]
    for i in range(depth):
        j = i // 2
        g_ffn, w_rt = p["ffn_norm"][i][None], p["w_router_t"][i]
        if i % 2 == 0:
            qkvs = _qkv_proj(x, p["attn_norm"][j][None], p["w_qkv"][j], tables, B, S)
            res = [_dilated_attention(qkvs[g], g) for g in range(N_GROUPS)]
            x, hn, aff = _attn_out_proj([r[0] for r in res], [r[1] for r in res], p["w_attn_out"][j], x,
                                        g_ffn, w_rt, S)
        else:
            x, hn, aff = _short_conv(x, p["conv_norm"][j][None], p["w_conv_in"][j], p["conv_w"][j],
                                     p["w_conv_out"][j], g_ffn, w_rt, S)
        final_g = p["final_norm"][None] if i == depth - 1 else None
        x = _moe(x, hn, aff, p["w_gate"], p["w_up"], p["w_down"], i, final_g)
    return x.reshape(B, S, D)


def kernel(x_prompt, x_sample, attn_norm, w_qkv, w_attn_out, conv_norm, w_conv_in, conv_w, w_conv_out,
           ffn_norm, w_router, w_gate, w_up, w_down, final_norm):
    per_layer = lambda w: [w[i].astype(BF16) for i in range(w.shape[0])]
    p = dict(
        attn_norm=attn_norm, conv_norm=conv_norm, ffn_norm=ffn_norm, final_norm=final_norm, conv_w=conv_w,
        w_qkv=per_layer(w_qkv), w_attn_out=per_layer(w_attn_out),
        w_conv_in=per_layer(w_conv_in), w_conv_out=per_layer(w_conv_out),
        w_router_t=per_layer(jnp.swapaxes(w_router, 1, 2)),
        w_gate=w_gate, w_up=w_up, w_down=w_down,
    )
    return (_trunk(x_prompt, p), _trunk(x_sample, p))
```

```python
import functools

import jax
import jax.numpy as jnp
from jax import lax
from jax.experimental import pallas as pl
from jax.experimental.pallas import tpu as pltpu

D_MODEL = 1024
HEAD_DIM = 64
HEADS_PER_GROUP = 4
DILATED_GROUPS = ((128, 1), (512, 4), (2048, 16))
N_GROUPS = len(DILATED_GROUPS)
ATTN_WIDTH = N_GROUPS * HEADS_PER_GROUP * HEAD_DIM
GROUP_WIDTH = HEADS_PER_GROUP * HEAD_DIM
ROPE_DIM = HEAD_DIM // 4
ROPE_THETA = 500000.0
N_EXPERTS = 16
EC_CAPACITY = 2
RMS_EPS = 1e-6
NEG_INF = -1e30

LANES = 128
SUB = 8
SUB_BF16 = 16
ROW_TILE = 512
ATTN_TQ = 128
MOE_TILE = 256
CHUNK = 256
PIECE = 16
WIDE = 64
LOG2_PIECE = 4
LOG2_CHUNK = 8
LOG2_SUB = 3
LOG2_WIDE_PIECES = 2
assert (1 << LOG2_PIECE, 1 << LOG2_CHUNK, 1 << LOG2_SUB) == (PIECE, CHUNK, SUB)
assert PIECE << LOG2_WIDE_PIECES == WIDE
VMEM_LIMIT = 56 * 1024 * 1024

F32 = jnp.float32
BF16 = jnp.bfloat16


def _params(n_axes):
    return pltpu.CompilerParams(
        dimension_semantics=("arbitrary",) * n_axes, vmem_limit_bytes=VMEM_LIMIT)


def _rms(x, g):
    ms = jnp.mean(x * x, axis=-1, keepdims=True)
    return x * lax.rsqrt(ms + RMS_EPS) * g


def _qkv_kernel(x_ref, g_ref, w_ref, cos_ref, sa_ref, sb_ref, o0_ref, o1_ref, o2_ref, s_ref):
    tm = x_ref.shape[0]
    h = _rms(x_ref[...], g_ref[...]).astype(BF16)
    acc = jnp.dot(h, w_ref[...], preferred_element_type=F32)
    c, sa, sb = cos_ref[...], sa_ref[...], sb_ref[...]
    n_q = ATTN_WIDTH // LANES
    for j in range(2 * n_q):
        blk = acc[:, j * LANES:(j + 1) * LANES]
        r = blk * c + pltpu.roll(blk, LANES - ROPE_DIM // 2, 1) * sa + pltpu.roll(blk, ROPE_DIM // 2, 1) * sb
        if j < n_q:
            r = r * (HEAD_DIM ** -0.5)
        s_ref[j] = r
    for j in range(2 * n_q, 3 * n_q):
        s_ref[j] = acc[:, j * LANES:(j + 1) * LANES]
    per_group = GROUP_WIDTH // LANES
    for g, o_ref in enumerate((o0_ref, o1_ref, o2_ref)):
        dil = DILATED_GROUPS[g][1]
        for part in range(3):
            for jj in range(per_group):
                src = part * n_q + g * per_group + jj
                dst = slice((part * per_group + jj) * LANES, (part * per_group + jj + 1) * LANES)
                for r in range(dil):
                    rows = pl.ds(r, tm // dil, stride=dil) if dil > 1 else slice(None)
                    o_ref[r, :, dst] = s_ref[src, rows, :].astype(BF16)


def _rope_tables(S):
    half = ROPE_DIM // 2
    inv_freq = jnp.power(ROPE_THETA, -jnp.arange(0, ROPE_DIM, 2, dtype=F32) / ROPE_DIM)
    ang = jnp.arange(S, dtype=F32)[:, None] * inv_freq[None, :]
    cos, sin = jnp.cos(ang), jnp.sin(ang)
    rest = HEAD_DIM - ROPE_DIM
    c = jnp.concatenate([cos, cos, jnp.ones((S, rest), F32)], axis=1)
    sa = jnp.concatenate([-sin, jnp.zeros((S, half + rest), F32)], axis=1)
    sb = jnp.concatenate([jnp.zeros((S, half), F32), sin, jnp.zeros((S, rest), F32)], axis=1)
    rep = LANES // HEAD_DIM
    return tuple(jnp.tile(t, (1, rep)) for t in (c, sa, sb))


def _qkv_proj(x, g, w, tables, B, S):
    T, D = x.shape
    N = w.shape[1]
    tm = ROW_TILE
    per_seq = S // tm
    tab_spec = pl.BlockSpec((tm, LANES), lambda i: (i % per_seq, 0))
    dils = [d for _, d in DILATED_GROUPS]
    return pl.pallas_call(
        _qkv_kernel,
        grid=(T // tm,),
        in_specs=[
            pl.BlockSpec((tm, D), lambda i: (i, 0)),
            pl.BlockSpec((1, D), lambda i: (0, 0)),
            pl.BlockSpec((D, N), lambda i: (0, 0)),
            tab_spec, tab_spec, tab_spec,
        ],
        out_specs=[pl.BlockSpec((None, d, tm // d, 3 * GROUP_WIDTH), lambda i: (i // per_seq, 0, i % per_seq, 0))
                   for d in dils],
        out_shape=[jax.ShapeDtypeStruct((B, d, S // d, 3 * GROUP_WIDTH), BF16) for d in dils],
        scratch_shapes=[pltpu.VMEM((N // LANES, tm, LANES), F32)],
        compiler_params=_params(1),
        name="qkv_proj",
    )(x, g, w, *tables)


def _attn_kernel(q_ref, k_ref, v_ref, o_ref, l_ref, *, L, Lq, kw, tq, radius):
    c = pl.program_id(2)
    n_blk = Lq // tq

    def body(idx, carry):
        rr = idx // n_blk
        i = idx % n_blk
        r0 = pl.multiple_of(i * tq, tq)
        t0 = c * Lq + i * tq
        ks = pl.multiple_of(jnp.clip(t0 - radius, 0, L - kw), radius)
        qb = q_ref[rr, pl.ds(r0, tq), :]
        kb = k_ref[rr, pl.ds(ks, kw), :]
        vb = v_ref[rr, pl.ds(ks, kw), :]
        qi = t0 + lax.broadcasted_iota(jnp.int32, (tq, kw), 0)
        kj = ks + lax.broadcasted_iota(jnp.int32, (tq, kw), 1)
        valid = jnp.abs(qi - kj) <= radius
        per_block = LANES // HEAD_DIM
        lane_head = lax.broadcasted_iota(jnp.int32, (1, LANES), 1) // HEAD_DIM
        for blk in range(GROUP_WIDTH // LANES):
            lanes = slice(blk * LANES, (blk + 1) * LANES)
            qp, kp, vp = qb[:, lanes], kb[:, lanes], vb[:, lanes]
            o_blk = jnp.zeros((tq, LANES), F32)
            lse_blk = jnp.zeros((tq, LANES), F32)
            for hh in range(per_block):
                mine = lane_head == hh
                s = lax.dot_general(jnp.where(mine, qp, jnp.zeros_like(qp)), kp, (((1,), (1,)), ((), ())),
                                    preferred_element_type=F32)
                s = jnp.where(valid, s, NEG_INF)
                m = jnp.max(s, axis=-1, keepdims=True)
                p = jnp.exp(s - m)
                l = jnp.sum(p, axis=-1, keepdims=True)
                o_blk = o_blk + jnp.dot(p.astype(BF16), jnp.where(mine, vp, jnp.zeros_like(vp)),
                                        preferred_element_type=F32) / l
                lse_blk = jnp.where(mine, m + jnp.log(l), lse_blk)
            o_ref[rr, pl.ds(r0, tq), lanes] = o_blk
            l_ref[rr, pl.ds(r0, tq), lanes] = lse_blk
        return carry

    n_it = q_ref.shape[0] * n_blk
    lax.fori_loop(0, n_it, body, 0, unroll=2 if n_it % 2 == 0 else 1)


def _dilated_attention(qkv, group):
    window, dil = DILATED_GROUPS[group]
    B, _, L, _ = qkv.shape
    radius = window // (2 * dil)
    tq = min(ATTN_TQ, L)
    kw = min(L, tq + 2 * radius)
    Lq = min(L, 1024)
    assert L % tq == 0 and L % Lq == 0 and Lq % tq == 0 and radius % 16 == 0 and (L - kw) % radius == 0
    kern = functools.partial(_attn_kernel, L=L, Lq=Lq, kw=kw, tq=tq, radius=radius)
    R = min(dil, max(1, ROW_TILE // L))
    assert dil % R == 0
    out_spec = pl.BlockSpec((None, R, Lq, GROUP_WIDTH), lambda b, r, c: (b, r, c, 0))
    return pl.pallas_call(
        kern,
        grid=(B, dil // R, L // Lq),
        in_specs=[
            pl.BlockSpec((None, R, Lq, GROUP_WIDTH), lambda b, r, c: (b, r, c, 0)),
            pl.BlockSpec((None, R, L, GROUP_WIDTH), lambda b, r, c: (b, r, 0, 1)),
            pl.BlockSpec((None, R, L, GROUP_WIDTH), lambda b, r, c: (b, r, 0, 2)),
        ],
        out_specs=[out_spec, out_spec],
        out_shape=[jax.ShapeDtypeStruct((B, dil, L, GROUP_WIDTH), F32)] * 2,
        compiler_params=_params(3),
        name=f"dilated_attn_g{group}",
    )(qkv, qkv, qkv)


def _oproj_kernel(o0_ref, o1_ref, o2_ref, l0_ref, l1_ref, l2_ref, w_ref, x_ref, gf_ref, wr_ref,
                  y_ref, hn_ref, aff_ref, o_scr, l_scr):
    tm = x_ref.shape[0]
    for g, (o_ref, l_ref) in enumerate(((o0_ref, l0_ref), (o1_ref, l1_ref), (o2_ref, l2_ref))):
        dil = DILATED_GROUPS[g][1]
        for r in range(dil):
            rows = pl.ds(r, tm // dil, stride=dil) if dil > 1 else slice(None)
            for jj in range(GROUP_WIDTH // LANES):
                lanes = slice(jj * LANES, (jj + 1) * LANES)
                o_scr[g, jj, rows, :] = o_ref[r, :, lanes]
                l_scr[g, jj, rows, :] = l_ref[r, :, lanes]
    zs = []
    for jj in range(GROUP_WIDTH // LANES):
        l0, l1, l2 = l_scr[0, jj], l_scr[1, jj], l_scr[2, jj]
        m = jnp.maximum(jnp.maximum(l0, l1), l2)
        e0, e1, e2 = jnp.exp(l0 - m), jnp.exp(l1 - m), jnp.exp(l2 - m)
        den = e0 + e1 + e2
        zs.append((o_scr[0, jj] * (e0 / den), o_scr[1, jj] * (e1 / den), o_scr[2, jj] * (e2 / den)))
    z = jnp.concatenate([zs[jj][g] for g in range(N_GROUPS) for jj in range(GROUP_WIDTH // LANES)], axis=1)
    y = x_ref[...] + jnp.dot(z.astype(BF16), w_ref[...], preferred_element_type=F32)
    y_ref[...] = y
    _route(y, gf_ref, wr_ref, hn_ref, aff_ref)


def _attn_out_proj(outs, lses, w, x, g_ffn, w_rt, S):
    T, D = x.shape
    tm = ROW_TILE
    per_seq = S // tm
    gspecs = [pl.BlockSpec((None, d, tm // d, GROUP_WIDTH), lambda i: (i // per_seq, 0, i % per_seq, 0))
              for _, d in DILATED_GROUPS]
    xspec = pl.BlockSpec((tm, D), lambda i: (i, 0))
    r_in, r_out, r_shapes = _route_specs(T, D, w_rt.shape[0], tm)
    return pl.pallas_call(
        _oproj_kernel,
        grid=(T // tm,),
        in_specs=gspecs + gspecs + [pl.BlockSpec((ATTN_WIDTH, D), lambda i: (0, 0)), xspec] + r_in,
        out_specs=[xspec] + r_out,
        out_shape=[jax.ShapeDtypeStruct((T, D), F32)] + r_shapes,
        scratch_shapes=[pltpu.VMEM((N_GROUPS, GROUP_WIDTH // LANES, tm, LANES), F32)] * 2,
        compiler_params=_params(1),
        name="attn_out_proj",
    )(*outs, *lses, w, x, g_ffn, w_rt)


def _route(x, gf_ref, wr_ref, hn_ref, aff_ref):
    h = _rms(x, gf_ref[...]).astype(BF16)
    hn_ref[...] = h
    logits = lax.dot_general(wr_ref[...], h, (((1,), (1,)), ((), ())),
                             preferred_element_type=F32)
    m = jnp.max(logits, axis=0, keepdims=True)
    e = jnp.exp(logits - m)
    aff = e / jnp.sum(e, axis=0, keepdims=True)
    for k in range(aff_ref.shape[0]):
        aff_ref[k] = aff[:, k * MOE_TILE:(k + 1) * MOE_TILE]


def _route_specs(T, D, E, tm):
    per = tm // MOE_TILE
    in_specs = [pl.BlockSpec((1, D), lambda i: (0, 0)), pl.BlockSpec((E, D), lambda i: (0, 0))]
    out_specs = [pl.BlockSpec((tm, D), lambda i: (i, 0)), pl.BlockSpec((per, E, MOE_TILE), lambda i: (i, 0, 0))]
    out_shapes = [jax.ShapeDtypeStruct((T, D), BF16), jax.ShapeDtypeStruct((T // MOE_TILE, E, MOE_TILE), F32)]
    return in_specs, out_specs, out_shapes


def _conv_kernel(x_ref, xp_ref, xn_ref, g_ref, win_ref, cw_ref, wout_ref, gf_ref, wr_ref,
                 y_ref, hn_ref, aff_ref, u_scr, *, tm, per_seq):
    i = pl.program_id(0)
    D = x_ref.shape[1]
    x = x_ref[...]
    xa = jnp.concatenate([xp_ref[...], x, xn_ref[...]], axis=0)
    h = _rms(xa, g_ref[...]).astype(BF16)
    full = jnp.dot(h, win_ref[...], preferred_element_type=F32)
    u_scr[...] = full[:, D:2 * D] * full[:, 2 * D:]
    first = (i % per_seq) == 0
    last = (i % per_seq) == per_seq - 1
    u_scr[pl.ds(7, 1), :] = jnp.where(first, 0.0, u_scr[pl.ds(7, 1), :])
    u_scr[pl.ds(tm + 8, 1), :] = jnp.where(last, 0.0, u_scr[pl.ds(tm + 8, 1), :])
    cw = cw_ref[...]
    conv = (cw[0:1] * u_scr[pl.ds(7, tm), :] + cw[1:2] * u_scr[pl.ds(8, tm), :]
            + cw[2:3] * u_scr[pl.ds(9, tm), :])
    z = (full[8:8 + tm, :D] * conv).astype(BF16)
    y = x + jnp.dot(z, wout_ref[...], preferred_element_type=F32)
    y_ref[...] = y
    _route(y, gf_ref, wr_ref, hn_ref, aff_ref)


def _short_conv(x, g, w_in, cw, w_out, g_ffn, w_rt, S):
    T, D = x.shape
    tm = ROW_TILE
    per_seq = S // tm
    sub = tm // 8
    n8 = T // 8
    kern = functools.partial(_conv_kernel, tm=tm, per_seq=per_seq)
    r_in, r_out, r_shapes = _route_specs(T, D, w_rt.shape[0], tm)
    return pl.pallas_call(
        kern,
        grid=(T // tm,),
        in_specs=[
            pl.BlockSpec((tm, D), lambda i: (i, 0)),
            pl.BlockSpec((8, D), lambda i: (jnp.maximum(i * sub - 1, 0), 0)),
            pl.BlockSpec((8, D), lambda i: (jnp.minimum((i + 1) * sub, n8 - 1), 0)),
            pl.BlockSpec((1, D), lambda i: (0, 0)),
            pl.BlockSpec((D, 3 * D), lambda i: (0, 0)),
            pl.BlockSpec((3, D), lambda i: (0, 0)),
            pl.BlockSpec((D, D), lambda i: (0, 0)),
        ] + r_in,
        out_specs=[pl.BlockSpec((tm, D), lambda i: (i, 0))] + r_out,
        out_shape=[jax.ShapeDtypeStruct((T, D), F32)] + r_shapes,
        scratch_shapes=[pltpu.VMEM((tm + 16, D), F32)],
        compiler_params=_params(1),
        name="short_conv",
    )(x, x, x, g, w_in, cw, w_out, g_ffn, w_rt)


def _strict_upper(n):
    r = lax.broadcasted_iota(jnp.int32, (n, n), 0)
    c = lax.broadcasted_iota(jnp.int32, (n, n), 1)
    return (r < c).astype(BF16)


def _select_kernel(aff_ref, sel_ref, cnt_ref, pos_ref, *, cap):
    NT, E, tm = aff_ref.shape
    capf = float(cap)

    def count_ge(v):
        bits = lax.bitcast_convert_type(aff_ref[...], jnp.int32)
        ge = (bits >= v[None]).astype(F32)
        return jnp.sum(jnp.sum(ge, axis=0), axis=1, keepdims=True)

    def bisect(_, lh):
        lo, hi = lh
        mid = lo + lax.shift_right_logical(hi - lo, 1)
        ok = count_ge(mid) >= capf
        return jnp.where(ok, mid, lo), jnp.where(ok, hi, mid)

    one_bits = 0x3F800000
    lo0 = jnp.zeros((E, 1), jnp.int32)
    hi0 = jnp.full((E, 1), one_bits + 1, jnp.int32)
    tau, _ = lax.fori_loop(0, 31, bisect, (lo0, hi0))
    need = capf - count_ge(tau + 1)
    tri = _strict_upper(tm)

    def tile(j, carry):
        ceq, csel = carry
        bits = lax.bitcast_convert_type(aff_ref[j], jnp.int32)
        gt = bits > tau
        eq = (bits == tau).astype(F32)
        rank_eq = jnp.dot(eq.astype(BF16), tri, preferred_element_type=F32) + ceq
        sel = jnp.where(gt | ((eq > 0) & (rank_eq < need)), 1.0, 0.0)
        sel_ref[j] = sel
        c = jnp.sum(sel, axis=1, keepdims=True)
        cnt_ref[j] = jnp.broadcast_to(c, (E, LANES))
        pos_ref[j] = jnp.broadcast_to(csel, (E, LANES))
        return ceq + jnp.sum(eq, axis=1, keepdims=True), csel + c

    zero = jnp.zeros((E, 1), F32)
    lax.fori_loop(0, NT, tile, (zero, zero))


def _select(aff, cap):
    NT, E, tm = aff.shape
    sel, cnt, pos = pl.pallas_call(
        functools.partial(_select_kernel, cap=cap),
        out_shape=[jax.ShapeDtypeStruct((NT, E, tm), F32),
                   jax.ShapeDtypeStruct((NT, E, LANES), F32),
                   jax.ShapeDtypeStruct((NT, E, LANES), F32)],
        compiler_params=pltpu.CompilerParams(vmem_limit_bytes=VMEM_LIMIT),
        name="expert_select",
    )(aff)
    to_smem = lambda a: a[:, :, 0].astype(jnp.int32).reshape(NT * E)
    return sel, to_smem(cnt), to_smem(pos)


def _stack_rows(E, tm, sub):
    worst = E * tm + E * (sub - 1 + PIECE - 1)
    return -(-worst // CHUNK) * CHUNK


def _cdiv_pow2(x, log2):
    return lax.shift_right_logical(x + ((1 << log2) - 1), log2)


def _segments(cnt_ref, pos_ref, tile, E, sub):
    segs = []
    base = jnp.int32(0)
    for e in range(E):
        n = cnt_ref[tile * E + e]
        p0 = pos_ref[tile * E + e]
        a = p0 & (sub - 1)
        pieces = jnp.where(n > 0, _cdiv_pow2(a + n, LOG2_PIECE), 0)
        segs.append((base, pieces, p0 - a, a, n))
        base = base + pieces * PIECE
    return segs, base


NO_RANK = -256.0
NO_OWNER = 1024.0


def _segment_rows(segs):
    lane = lax.broadcasted_iota(jnp.int32, (1, LANES), 1)
    first = jnp.full((1, LANES), float(NO_OWNER * NO_OWNER), F32)
    end = jnp.full((1, LANES), -1.0, F32)
    zero = jnp.zeros((1, LANES), F32)
    for e, (base, pieces, _, a, _) in enumerate(segs):
        m = lane == e
        first = jnp.where(m, base.astype(F32), first)
        end = jnp.where(m, (base + pieces * PIECE).astype(F32), end)
        zero = jnp.where(m, (base + a).astype(F32), zero)
    return first, end, zero


def _tile_ranks(sel, tm):
    E = sel.shape[0]
    rank = jnp.dot(sel.astype(BF16), _strict_upper(tm), preferred_element_type=F32)
    rank = jnp.where(sel > 0, rank, NO_RANK).astype(BF16)
    return jnp.concatenate([rank, jnp.zeros((LANES - E, tm), BF16)], axis=0)


def _chunk_one_hot(j0, rows, ranks, E):
    first, end, zero = rows
    jv = (j0 + lax.broadcasted_iota(jnp.int32, (CHUNK, LANES), 0)).astype(F32)
    own = (jv >= first) & (jv < end)
    lane = lax.broadcasted_iota(jnp.int32, (CHUNK, LANES), 1)
    tgt = jnp.sum(jnp.where(own & (lane < E), jv - zero + NO_OWNER, 0.0), axis=1, keepdims=True) - NO_OWNER
    ownb = jnp.where(own, 1.0, 0.0).astype(BF16)
    hit = jnp.dot(ownb, ranks, preferred_element_type=F32) == tgt
    return ownb, hit


def _total_pieces(segs):
    t = segs[0][1]
    for s in segs[1:]:
        t = t + s[1]
    return t


WAIT_GROUP = 8
LOG2_WAIT_GROUP = 3
assert 1 << LOG2_WAIT_GROUP == WAIT_GROUP


def _wait_pieces(n, copy_of_rows, unit=PIECE):
    def waits(count, rows):
        def w(_, c):
            copy_of_rows(rows).wait()
            return c
        lax.fori_loop(0, count, w, 0)
    waits(lax.shift_right_logical(n, LOG2_WAIT_GROUP), WAIT_GROUP * unit)
    waits(n & (WAIT_GROUP - 1), unit)


def _dispatch_kernel(cnt_ref, pos_ref, sel_ref, hn_ref, xe_ref, stack, pend, prev_copies, sem):
    i = pl.program_id(0)
    nt = pl.num_programs(0)
    slot = i % 2
    E, tm = sel_ref.shape

    @pl.when(i == 0)
    def _():
        pend[...] = jnp.zeros_like(pend)
        stack[...] = jnp.zeros_like(stack)

    segs, total = _segments(cnt_ref, pos_ref, i, E, SUB)
    rows = _segment_rows(segs)
    ranks = _tile_ranks(sel_ref[...], tm)

    def chunk(k, c):
        j0 = pl.multiple_of(k * CHUNK, CHUNK)
        _, hit = _chunk_one_hot(j0, rows, ranks, E)
        oh = jnp.where(hit, 1.0, 0.0).astype(BF16)
        stack[slot, pl.ds(j0, CHUNK), :] = jnp.dot(oh, hn_ref[...], preferred_element_type=F32)
        return c

    lax.fori_loop(0, _cdiv_pow2(total, LOG2_CHUNK), chunk, 0)

    spare = stack.shape[1] - PIECE
    for e in range(E):
        base, _, _, a, n = segs[e]
        live = n > 0
        first = pl.multiple_of(jnp.where(live, base, spare), SUB)
        last = pl.multiple_of(
            jnp.where(live, base + lax.shift_right_logical(a + n - 1, LOG2_SUB) * SUB, spare), SUB)
        stack[slot, pl.ds(first, SUB), :] = stack[slot, pl.ds(first, SUB), :] + jnp.where(live, pend[e], 0.0)
        incomplete = live & (((a + n) & (SUB - 1)) != 0)
        pend[e] = jnp.where(live, jnp.where(incomplete, stack[slot, pl.ds(last, SUB), :], 0.0), pend[e])

    def wide_copy(s, src_row, e, dst_row, rows=WIDE):
        return pltpu.make_async_copy(stack.at[s, pl.ds(src_row, rows), :],
                                     xe_ref.at[e, pl.ds(dst_row, rows), :], sem.at[s])

    def copies_of(some_segs):
        t = jnp.int32(0)
        for _, pieces, _, _, _ in some_segs:
            t = t + _cdiv_pow2(pieces, LOG2_WIDE_PIECES)
        return t

    waiter = lambda s: (lambda rows: wide_copy(s, 0, 0, 0, rows))

    n_copies = copies_of(segs)

    @pl.when(i > 0)
    def _():
        _wait_pieces(prev_copies[0], waiter(1 - slot), WIDE)

    prev_copies[0] = n_copies

    for e in range(E):
        base, pieces, start, _, _ = segs[e]

        def issue(c, carry, base=base, start=start, e=e):
            wide_copy(slot, pl.multiple_of(base + c * WIDE, PIECE), e,
                      pl.multiple_of(start + c * WIDE, SUB)).start()
            return carry

        lax.fori_loop(0, _cdiv_pow2(pieces, LOG2_WIDE_PIECES), issue, 0)

    @pl.when(i == nt - 1)
    def _():
        _wait_pieces(n_copies, waiter(slot), WIDE)
        cap = xe_ref.shape[1] - WIDE
        stack[slot, pl.ds(0, WIDE), :] = jnp.zeros((WIDE, stack.shape[2]), F32)
        for e in range(E):
            wide_copy(slot, 0, e, cap).start()
        _wait_pieces(E, waiter(slot), WIDE)


def _dispatch(sel, cnt, pos, hn, cap):
    NT, E, tm = sel.shape
    T, D = hn.shape
    return pl.pallas_call(
        _dispatch_kernel,
        grid_spec=pltpu.PrefetchScalarGridSpec(
            num_scalar_prefetch=2,
            grid=(NT,),
            in_specs=[pl.BlockSpec((None, E, tm), lambda i, c, p: (i, 0, 0)),
                      pl.BlockSpec((tm, D), lambda i, c, p: (i, 0))],
            out_specs=pl.BlockSpec(memory_space=pl.ANY),
            scratch_shapes=[pltpu.VMEM((2, _stack_rows(E, tm, SUB) + WIDE, D), F32),
                            pltpu.VMEM((E, SUB, D), F32),
                            pltpu.SMEM((1,), jnp.int32),
                            pltpu.SemaphoreType.DMA((2,))]),
        out_shape=jax.ShapeDtypeStruct((E, cap + WIDE, D), F32),
        compiler_params=_params(1),
        name="expert_dispatch",
    )(cnt, pos, sel, hn)


def _segment_cols(segs):
    E = len(segs)
    eidx = lax.broadcasted_iota(jnp.int32, (E, 1), 0)
    first = jnp.zeros((E, 1), F32)
    end = jnp.zeros((E, 1), F32)
    zero = jnp.zeros((E, 1), F32)
    for e, (base, pieces, _, a, _) in enumerate(segs):
        m = eidx == e
        first = jnp.where(m, base.astype(F32), first)
        end = jnp.where(m, (base + pieces * PIECE).astype(F32), end)
        zero = jnp.where(m, (base + a).astype(F32), zero)
    return first, end, zero


def _combine_kernel(cnt_ref, pos_ref, sel_ref, aff_ref, x_ref, y_hbm, *rest):
    g_ref = rest[0] if len(rest) == 4 else None
    o_ref, stack, sem = rest[-3:]
    i = pl.program_id(0)
    nt = pl.num_programs(0)
    slot = i % 2
    E, tm = sel_ref.shape

    def piece_copy(s, e, src_row, dst_row, rows=PIECE):
        return pltpu.make_async_copy(y_hbm.at[e, pl.ds(src_row, rows), :],
                                     stack.at[s, pl.ds(dst_row, rows), :], sem.at[s])

    def fetch(tile, s):
        segs, _ = _segments(cnt_ref, pos_ref, tile, E, SUB_BF16)
        for e in range(E):
            base, pieces, start, _, _ = segs[e]

            def issue(c, carry, base=base, start=start, e=e):
                piece_copy(s, e, pl.multiple_of(start + c * PIECE, SUB_BF16),
                           pl.multiple_of(base + c * PIECE, PIECE)).start()
                return carry

            lax.fori_loop(0, pieces, issue, 0)

    @pl.when(i == 0)
    def _():
        stack[...] = jnp.zeros_like(stack)
        fetch(0, 0)

    @pl.when(i + 1 < nt)
    def _():
        fetch(i + 1, 1 - slot)

    segs, total = _segments(cnt_ref, pos_ref, i, E, SUB_BF16)
    first, end, zero = _segment_cols(segs)
    sel = sel_ref[...]
    rank = jnp.dot(sel.astype(BF16), _strict_upper(tm), preferred_element_type=F32)
    rank = jnp.where(sel > 0, rank, NO_RANK)
    pad = jnp.zeros((LANES - 2 * E, tm), F32)
    by_token = jnp.transpose(jnp.concatenate([aff_ref[...], rank, pad], axis=0)).astype(BF16)
    o_ref[...] = x_ref[...]
    _wait_pieces(_total_pieces(segs), lambda rows: piece_copy(slot, 0, 0, 0, rows))

    def chunk(k, c):
        j0 = pl.multiple_of(k * CHUNK, CHUNK)
        jl = (j0 + lax.broadcasted_iota(jnp.int32, (E, CHUNK), 1)).astype(F32)
        own = (jl >= first) & (jl < end)
        tgt = jnp.sum(jnp.where(own, jl - zero + NO_OWNER, 0.0), axis=0, keepdims=True) - NO_OWNER
        o16 = jnp.where(own, 1.0, 0.0).astype(BF16)
        zeros = lambda n: jnp.zeros((n, CHUNK), BF16)
        own_aff = jnp.concatenate([o16, zeros(LANES - E)], axis=0)
        own_rank = jnp.concatenate([zeros(E), o16, zeros(LANES - 2 * E)], axis=0)
        hit = jnp.dot(by_token, own_rank, preferred_element_type=F32) == tgt
        g = jnp.where(hit, jnp.dot(by_token, own_aff, preferred_element_type=F32), 0.0).astype(BF16)
        o_ref[...] += jnp.dot(g, stack[slot, pl.ds(j0, CHUNK), :], preferred_element_type=F32)
        return c

    lax.fori_loop(0, _cdiv_pow2(total, LOG2_CHUNK), chunk, 0)
    if g_ref is not None:
        o_ref[...] = _rms(o_ref[...], g_ref[...])


def _combine(sel, aff, cnt, pos, x, y, final_g=None):
    NT, E, tm = sel.shape
    T, D = x.shape
    tile_spec = pl.BlockSpec((None, E, tm), lambda i, c, p: (i, 0, 0))
    x_spec = pl.BlockSpec((tm, D), lambda i, c, p: (i, 0))
    extra_specs = [] if final_g is None else [pl.BlockSpec((1, D), lambda i, c, p: (0, 0))]
    extra = [] if final_g is None else [final_g]
    return pl.pallas_call(
        _combine_kernel,
        grid_spec=pltpu.PrefetchScalarGridSpec(
            num_scalar_prefetch=2,
            grid=(NT,),
            in_specs=[tile_spec, tile_spec, x_spec, pl.BlockSpec(memory_space=pl.ANY)] + extra_specs,
            out_specs=x_spec,
            scratch_shapes=[pltpu.VMEM((2, _stack_rows(E, tm, SUB_BF16), D), BF16),
                            pltpu.SemaphoreType.DMA((2,))]),
        out_shape=jax.ShapeDtypeStruct((T, D), F32),
        compiler_params=_params(1),
        name="expert_combine" if final_g is None else "expert_combine_norm",
    )(cnt, pos, sel, aff, x, y, *extra)


def _ffn_kernel(x_ref, wg_hbm, wu_hbm, wd_hbm, y_ref, wg_buf, wu_buf, wd_buf, wg_stg, wu_stg, wd_stg, sem,
                *, layer):
    e = pl.program_id(0)
    i = pl.program_id(1)
    n_e = pl.num_programs(0)
    n = pl.num_programs(1) - 1
    slot = e % 2
    stages = ((wg_hbm, wg_stg, wg_buf), (wu_hbm, wu_stg, wu_buf), (wd_hbm, wd_stg, wd_buf))

    def chunk_copy(k, expert, c):
        hbm, stg, _ = stages[k]
        rows = stg.shape[0]
        return pltpu.make_async_copy(hbm.at[layer, expert, pl.ds(pl.multiple_of(c * rows, rows), rows), :],
                                     stg, sem.at[k])

    def land(expert, c, dst_slot):
        for k, (_, stg, buf) in enumerate(stages):
            rows = stg.shape[0]
            chunk_copy(k, expert, c).wait()
            buf[dst_slot, pl.ds(pl.multiple_of(c * rows, rows), rows), :] = stg[...].astype(BF16)

    @pl.when((e == 0) & (i == 0))
    def _():
        def first(c, carry):
            for k in range(3):
                chunk_copy(k, 0, c).start()
            land(0, c, 0)
            return carry
        lax.fori_loop(0, n, first, 0)

    nxt = jnp.minimum(e + 1, n_e - 1)

    @pl.when((e + 1 < n_e) & (i > 0))
    def _():
        land(nxt, i - 1, 1 - slot)

    @pl.when((e + 1 < n_e) & (i < n))
    def _():
        for k in range(3):
            chunk_copy(k, nxt, i).start()

    @pl.when(i < n)
    def _():
        x = x_ref[...].astype(BF16)
        a = jnp.dot(x, wg_buf[slot], preferred_element_type=F32)
        b = jnp.dot(x, wu_buf[slot], preferred_element_type=F32)
        h = (a * (1.0 / (1.0 + jnp.exp(-a)))) * b
        y_ref[...] = jnp.dot(h.astype(BF16), wd_buf[slot], preferred_element_type=F32).astype(y_ref.dtype)

    @pl.when(i == n)
    def _():
        y_ref[...] = jnp.zeros_like(y_ref)


def _expert_ffn(xe, cap, wg, wu, wd, layer):
    E, _, D = xe.shape
    F = wg.shape[3]
    tm = min(ROW_TILE, cap)
    n = cap // tm
    any_spec = pl.BlockSpec(memory_space=pl.ANY)
    return pl.pallas_call(
        functools.partial(_ffn_kernel, layer=layer),
        grid=(E, n + 1),
        in_specs=[pl.BlockSpec((None, tm, D), lambda e, i: (e, jnp.minimum(i, n - 1), 0)),
                  any_spec, any_spec, any_spec],
        out_specs=pl.BlockSpec((None, tm, D), lambda e, i: (e, i, 0)),
        out_shape=jax.ShapeDtypeStruct((E, cap + tm, D), BF16),
        scratch_shapes=[pltpu.VMEM((2, D, F), BF16), pltpu.VMEM((2, D, F), BF16), pltpu.VMEM((2, F, D), BF16),
                        pltpu.VMEM((D // n, F), F32), pltpu.VMEM((D // n, F), F32), pltpu.VMEM((F // n, D), F32),
                        pltpu.SemaphoreType.DMA((3,))],
        compiler_params=_params(2),
        name=f"expert_ffn_cap{cap}",
    )(xe, wg, wu, wd)


def _moe(x, hn, aff, wg, wu, wd, layer, final_g):
    T, D = x.shape
    cap = max(1, (EC_CAPACITY * T) // N_EXPERTS)
    sel, cnt, pos = _select(aff, cap)
    xe = _dispatch(sel, cnt, pos, hn, cap)
    y = _expert_ffn(xe, cap, wg, wu, wd, layer)
    return _combine(sel, aff, cnt, pos, x, y, final_g)


def _trunk(x3, p):
    B, S, D = x3.shape
    x = x3.reshape(B * S, D)
    tables = _rope_tables(S)
    depth = p["ffn_norm"].shape[0]
    for i in range(depth):
        j = i // 2
        g_ffn, w_rt = p["ffn_norm"][i][None], p["w_router_t"][i]
        if i % 2 == 0:
            qkvs = _qkv_proj(x, p["attn_norm"][j][None], p["w_qkv"][j], tables, B, S)
            res = [_dilated_attention(qkvs[g], g) for g in range(N_GROUPS)]
            x, hn, aff = _attn_out_proj([r[0] for r in res], [r[1] for r in res], p["w_attn_out"][j], x,
                                        g_ffn, w_rt, S)
        else:
            x, hn, aff = _short_conv(x, p["conv_norm"][j][None], p["w_conv_in"][j], p["conv_w"][j],
                                     p["w_conv_out"][j], g_ffn, w_rt, S)
        final_g = p["final_norm"][None] if i == depth - 1 else None
        x = _moe(x, hn, aff, p["w_gate"], p["w_up"], p["w_down"], i, final_g)
    return x.reshape(B, S, D)


def kernel(x_prompt, x_sample, attn_norm, w_qkv, w_attn_out, conv_norm, w_conv_in, conv_w, w_conv_out,
           ffn_norm, w_router, w_gate, w_up, w_down, final_norm):
    per_layer = lambda w: [w[i].astype(BF16) for i in range(w.shape[0])]
    p = dict(
        attn_norm=attn_norm, conv_norm=conv_norm, ffn_norm=ffn_norm, final_norm=final_norm, conv_w=conv_w,
        w_qkv=per_layer(w_qkv), w_attn_out=per_layer(w_attn_out),
        w_conv_in=per_layer(w_conv_in), w_conv_out=per_layer(w_conv_out),
        w_router_t=per_layer(jnp.swapaxes(w_router, 1, 2)),
        w_gate=w_gate, w_up=w_up, w_down=w_down,
    )
    return (_trunk(x_prompt, p), _trunk(x_sample, p))
```

```python
import functools

import jax
import jax.numpy as jnp
from jax import lax
from jax.experimental import pallas as pl
from jax.experimental.pallas import tpu as pltpu

D_MODEL = 1024
HEAD_DIM = 64
HEADS_PER_GROUP = 4
DILATED_GROUPS = ((128, 1), (512, 4), (2048, 16))
N_GROUPS = len(DILATED_GROUPS)
ATTN_WIDTH = N_GROUPS * HEADS_PER_GROUP * HEAD_DIM
GROUP_WIDTH = HEADS_PER_GROUP * HEAD_DIM
ROPE_DIM = HEAD_DIM // 4
ROPE_THETA = 500000.0
N_EXPERTS = 16
EC_CAPACITY = 2
RMS_EPS = 1e-6
NEG_INF = -1e30

LANES = 128
SUB = 8
SUB_BF16 = 16
ROW_TILE = 512
ATTN_TQ = 128
MOE_TILE = 256
CHUNK = 256
PIECE = 16
WIDE = 64
LOG2_PIECE = 4
LOG2_CHUNK = 8
LOG2_SUB = 3
LOG2_WIDE_PIECES = 2
assert (1 << LOG2_PIECE, 1 << LOG2_CHUNK, 1 << LOG2_SUB) == (PIECE, CHUNK, SUB)
assert PIECE << LOG2_WIDE_PIECES == WIDE
VMEM_LIMIT = 56 * 1024 * 1024

F32 = jnp.float32
BF16 = jnp.bfloat16


def _params(n_axes):
    return pltpu.CompilerParams(
        dimension_semantics=("arbitrary",) * n_axes, vmem_limit_bytes=VMEM_LIMIT)


def _rms(x, g):
    ms = jnp.mean(x * x, axis=-1, keepdims=True)
    return x * lax.rsqrt(ms + RMS_EPS) * g


def _qkv_kernel(x_ref, g_ref, w_ref, cos_ref, sa_ref, sb_ref, o0_ref, o1_ref, o2_ref, s_ref):
    tm = x_ref.shape[0]
    h = _rms(x_ref[...], g_ref[...]).astype(BF16)
    acc = jnp.dot(h, w_ref[...], preferred_element_type=F32)
    c, sa, sb = cos_ref[...], sa_ref[...], sb_ref[...]
    n_q = ATTN_WIDTH // LANES
    for j in range(2 * n_q):
        blk = acc[:, j * LANES:(j + 1) * LANES]
        r = blk * c + pltpu.roll(blk, LANES - ROPE_DIM // 2, 1) * sa + pltpu.roll(blk, ROPE_DIM // 2, 1) * sb
        if j < n_q:
            r = r * (HEAD_DIM ** -0.5)
        s_ref[j] = r
    for j in range(2 * n_q, 3 * n_q):
        s_ref[j] = acc[:, j * LANES:(j + 1) * LANES]
    per_group = GROUP_WIDTH // LANES
    for g, o_ref in enumerate((o0_ref, o1_ref, o2_ref)):
        dil = DILATED_GROUPS[g][1]
        for part in range(3):
            for jj in range(per_group):
                src = part * n_q + g * per_group + jj
                dst = slice((part * per_group + jj) * LANES, (part * per_group + jj + 1) * LANES)
                for r in range(dil):
                    rows = pl.ds(r, tm // dil, stride=dil) if dil > 1 else slice(None)
                    o_ref[r, :, dst] = s_ref[src, rows, :].astype(BF16)


def _rope_tables(S):
    half = ROPE_DIM // 2
    inv_freq = jnp.power(ROPE_THETA, -jnp.arange(0, ROPE_DIM, 2, dtype=F32) / ROPE_DIM)
    ang = jnp.arange(S, dtype=F32)[:, None] * inv_freq[None, :]
    cos, sin = jnp.cos(ang), jnp.sin(ang)
    rest = HEAD_DIM - ROPE_DIM
    c = jnp.concatenate([cos, cos, jnp.ones((S, rest), F32)], axis=1)
    sa = jnp.concatenate([-sin, jnp.zeros((S, half + rest), F32)], axis=1)
    sb = jnp.concatenate([jnp.zeros((S, half), F32), sin, jnp.zeros((S, rest), F32)], axis=1)
    rep = LANES // HEAD_DIM
    return tuple(jnp.tile(t, (1, rep)) for t in (c, sa, sb))


def _qkv_proj(x, g, w, tables, B, S):
    T, D = x.shape
    N = w.shape[1]
    tm = ROW_TILE
    per_seq = S // tm
    tab_spec = pl.BlockSpec((tm, LANES), lambda i: (i % per_seq, 0))
    dils = [d for _, d in DILATED_GROUPS]
    return pl.pallas_call(
        _qkv_kernel,
        grid=(T // tm,),
        in_specs=[
            pl.BlockSpec((tm, D), lambda i: (i, 0)),
            pl.BlockSpec((1, D), lambda i: (0, 0)),
            pl.BlockSpec((D, N), lambda i: (0, 0)),
            tab_spec, tab_spec, tab_spec,
        ],
        out_specs=[pl.BlockSpec((None, d, tm // d, 3 * GROUP_WIDTH), lambda i: (i // per_seq, 0, i % per_seq, 0))
                   for d in dils],
        out_shape=[jax.ShapeDtypeStruct((B, d, S // d, 3 * GROUP_WIDTH), BF16) for d in dils],
        scratch_shapes=[pltpu.VMEM((N // LANES, tm, LANES), F32)],
        compiler_params=_params(1),
        name="qkv_proj",
    )(x, g, w, *tables)


def _attn_kernel(q_ref, k_ref, v_ref, o_ref, l_ref, *, L, Lq, kw, tq, radius):
    c = pl.program_id(2)
    n_blk = Lq // tq

    def body(idx, carry):
        rr = idx // n_blk
        i = idx % n_blk
        r0 = pl.multiple_of(i * tq, tq)
        t0 = c * Lq + i * tq
        ks = pl.multiple_of(jnp.clip(t0 - radius, 0, L - kw), radius)
        qb = q_ref[rr, pl.ds(r0, tq), :]
        kb = k_ref[rr, pl.ds(ks, kw), :]
        vb = v_ref[rr, pl.ds(ks, kw), :]
        qi = t0 + lax.broadcasted_iota(jnp.int32, (tq, kw), 0)
        kj = ks + lax.broadcasted_iota(jnp.int32, (tq, kw), 1)
        valid = jnp.abs(qi - kj) <= radius
        per_block = LANES // HEAD_DIM
        lane_head = lax.broadcasted_iota(jnp.int32, (1, LANES), 1) // HEAD_DIM
        for blk in range(GROUP_WIDTH // LANES):
            lanes = slice(blk * LANES, (blk + 1) * LANES)
            qp, kp, vp = qb[:, lanes], kb[:, lanes], vb[:, lanes]
            o_blk = jnp.zeros((tq, LANES), F32)
            lse_blk = jnp.zeros((tq, LANES), F32)
            for hh in range(per_block):
                mine = lane_head == hh
                s = lax.dot_general(jnp.where(mine, qp, jnp.zeros_like(qp)), kp, (((1,), (1,)), ((), ())),
                                    preferred_element_type=F32)
                s = jnp.where(valid, s, NEG_INF)
                m = jnp.max(s, axis=-1, keepdims=True)
                p = jnp.exp(s - m)
                l = jnp.sum(p, axis=-1, keepdims=True)
                o_blk = o_blk + jnp.dot(p.astype(BF16), jnp.where(mine, vp, jnp.zeros_like(vp)),
                                        preferred_element_type=F32) / l
                lse_blk = jnp.where(mine, m + jnp.log(l), lse_blk)
            o_ref[rr, pl.ds(r0, tq), lanes] = o_blk
            l_ref[rr, pl.ds(r0, tq), lanes] = lse_blk
        return carry

    n_it = q_ref.shape[0] * n_blk
    lax.fori_loop(0, n_it, body, 0, unroll=2 if n_it % 2 == 0 else 1)


def _dilated_attention(qkv, group):
    window, dil = DILATED_GROUPS[group]
    B, _, L, _ = qkv.shape
    radius = window // (2 * dil)
    tq = min(ATTN_TQ, L)
    kw = min(L, tq + 2 * radius)
    Lq = min(L, 1024)
    assert L % tq == 0 and L % Lq == 0 and Lq % tq == 0 and radius % 16 == 0 and (L - kw) % radius == 0
    kern = functools.partial(_attn_kernel, L=L, Lq=Lq, kw=kw, tq=tq, radius=radius)
    R = min(dil, max(1, ROW_TILE // L))
    assert dil % R == 0
    out_spec = pl.BlockSpec((None, R, Lq, GROUP_WIDTH), lambda b, r, c: (b, r, c, 0))
    return pl.pallas_call(
        kern,
        grid=(B, dil // R, L // Lq),
        in_specs=[
            pl.BlockSpec((None, R, Lq, GROUP_WIDTH), lambda b, r, c: (b, r, c, 0)),
            pl.BlockSpec((None, R, L, GROUP_WIDTH), lambda b, r, c: (b, r, 0, 1)),
            pl.BlockSpec((None, R, L, GROUP_WIDTH), lambda b, r, c: (b, r, 0, 2)),
        ],
        out_specs=[out_spec, out_spec],
        out_shape=[jax.ShapeDtypeStruct((B, dil, L, GROUP_WIDTH), F32)] * 2,
        compiler_params=_params(3),
        name=f"dilated_attn_g{group}",
    )(qkv, qkv, qkv)


def _oproj_kernel(o0_ref, o1_ref, o2_ref, l0_ref, l1_ref, l2_ref, w_ref, x_ref, gf_ref, wr_ref,
                  y_ref, hn_ref, aff_ref, o_scr, l_scr):
    tm = x_ref.shape[0]
    for g, (o_ref, l_ref) in enumerate(((o0_ref, l0_ref), (o1_ref, l1_ref), (o2_ref, l2_ref))):
        dil = DILATED_GROUPS[g][1]
        for r in range(dil):
            rows = pl.ds(r, tm // dil, stride=dil) if dil > 1 else slice(None)
            for jj in range(GROUP_WIDTH // LANES):
                lanes = slice(jj * LANES, (jj + 1) * LANES)
                o_scr[g, jj, rows, :] = o_ref[r, :, lanes]
                l_scr[g, jj, rows, :] = l_ref[r, :, lanes]
    zs = []
    for jj in range(GROUP_WIDTH // LANES):
        l0, l1, l2 = l_scr[0, jj], l_scr[1, jj], l_scr[2, jj]
        m = jnp.maximum(jnp.maximum(l0, l1), l2)
        e0, e1, e2 = jnp.exp(l0 - m), jnp.exp(l1 - m), jnp.exp(l2 - m)
        den = e0 + e1 + e2
        zs.append((o_scr[0, jj] * (e0 / den), o_scr[1, jj] * (e1 / den), o_scr[2, jj] * (e2 / den)))
    z = jnp.concatenate([zs[jj][g] for g in range(N_GROUPS) for jj in range(GROUP_WIDTH // LANES)], axis=1)
    y = x_ref[...] + jnp.dot(z.astype(BF16), w_ref[...], preferred_element_type=F32)
    y_ref[...] = y
    _route(y, gf_ref, wr_ref, hn_ref, aff_ref)


def _attn_out_proj(outs, lses, w, x, g_ffn, w_rt, S):
    T, D = x.shape
    tm = ROW_TILE
    per_seq = S // tm
    gspecs = [pl.BlockSpec((None, d, tm // d, GROUP_WIDTH), lambda i: (i // per_seq, 0, i % per_seq, 0))
              for _, d in DILATED_GROUPS]
    xspec = pl.BlockSpec((tm, D), lambda i: (i, 0))
    r_in, r_out, r_shapes = _route_specs(T, D, w_rt.shape[0], tm)
    return pl.pallas_call(
        _oproj_kernel,
        grid=(T // tm,),
        in_specs=gspecs + gspecs + [pl.BlockSpec((ATTN_WIDTH, D), lambda i: (0, 0)), xspec] + r_in,
        out_specs=[xspec] + r_out,
        out_shape=[jax.ShapeDtypeStruct((T, D), F32)] + r_shapes,
        scratch_shapes=[pltpu.VMEM((N_GROUPS, GROUP_WIDTH // LANES, tm, LANES), F32)] * 2,
        compiler_params=_params(1),
        name="attn_out_proj",
    )(*outs, *lses, w, x, g_ffn, w_rt)


def _route(x, gf_ref, wr_ref, hn_ref, aff_ref):
    h = _rms(x, gf_ref[...]).astype(BF16)
    hn_ref[...] = h
    logits = lax.dot_general(wr_ref[...], h, (((1,), (1,)), ((), ())),
                             preferred_element_type=F32)
    m = jnp.max(logits, axis=0, keepdims=True)
    e = jnp.exp(logits - m)
    aff = e / jnp.sum(e, axis=0, keepdims=True)
    for k in range(aff_ref.shape[0]):
        aff_ref[k] = aff[:, k * MOE_TILE:(k + 1) * MOE_TILE]


def _route_specs(T, D, E, tm):
    per = tm // MOE_TILE
    in_specs = [pl.BlockSpec((1, D), lambda i: (0, 0)), pl.BlockSpec((E, D), lambda i: (0, 0))]
    out_specs = [pl.BlockSpec((tm, D), lambda i: (i, 0)), pl.BlockSpec((per, E, MOE_TILE), lambda i: (i, 0, 0))]
    out_shapes = [jax.ShapeDtypeStruct((T, D), BF16), jax.ShapeDtypeStruct((T // MOE_TILE, E, MOE_TILE), F32)]
    return in_specs, out_specs, out_shapes


def _conv_kernel(x_ref, xp_ref, xn_ref, g_ref, win_ref, cw_ref, wout_ref, gf_ref, wr_ref,
                 y_ref, hn_ref, aff_ref, u_scr, *, tm, per_seq):
    i = pl.program_id(0)
    D = x_ref.shape[1]
    x = x_ref[...]
    xa = jnp.concatenate([xp_ref[...], x, xn_ref[...]], axis=0)
    h = _rms(xa, g_ref[...]).astype(BF16)
    full = jnp.dot(h, win_ref[...], preferred_element_type=F32)
    u_scr[...] = full[:, D:2 * D] * full[:, 2 * D:]
    first = (i % per_seq) == 0
    last = (i % per_seq) == per_seq - 1
    u_scr[pl.ds(7, 1), :] = jnp.where(first, 0.0, u_scr[pl.ds(7, 1), :])
    u_scr[pl.ds(tm + 8, 1), :] = jnp.where(last, 0.0, u_scr[pl.ds(tm + 8, 1), :])
    cw = cw_ref[...]
    conv = (cw[0:1] * u_scr[pl.ds(7, tm), :] + cw[1:2] * u_scr[pl.ds(8, tm), :]
            + cw[2:3] * u_scr[pl.ds(9, tm), :])
    z = (full[8:8 + tm, :D] * conv).astype(BF16)
    y = x + jnp.dot(z, wout_ref[...], preferred_element_type=F32)
    y_ref[...] = y
    _route(y, gf_ref, wr_ref, hn_ref, aff_ref)


def _short_conv(x, g, w_in, cw, w_out, g_ffn, w_rt, S):
    T, D = x.shape
    tm = ROW_TILE
    per_seq = S // tm
    sub = tm // 8
    n8 = T // 8
    kern = functools.partial(_conv_kernel, tm=tm, per_seq=per_seq)
    r_in, r_out, r_shapes = _route_specs(T, D, w_rt.shape[0], tm)
    return pl.pallas_call(
        kern,
        grid=(T // tm,),
        in_specs=[
            pl.BlockSpec((tm, D), lambda i: (i, 0)),
            pl.BlockSpec((8, D), lambda i: (jnp.maximum(i * sub - 1, 0), 0)),
            pl.BlockSpec((8, D), lambda i: (jnp.minimum((i + 1) * sub, n8 - 1), 0)),
            pl.BlockSpec((1, D), lambda i: (0, 0)),
            pl.BlockSpec((D, 3 * D), lambda i: (0, 0)),
            pl.BlockSpec((3, D), lambda i: (0, 0)),
            pl.BlockSpec((D, D), lambda i: (0, 0)),
        ] + r_in,
        out_specs=[pl.BlockSpec((tm, D), lambda i: (i, 0))] + r_out,
        out_shape=[jax.ShapeDtypeStruct((T, D), F32)] + r_shapes,
        scratch_shapes=[pltpu.VMEM((tm + 16, D), F32)],
        compiler_params=_params(1),
        name="short_conv",
    )(x, x, x, g, w_in, cw, w_out, g_ffn, w_rt)


def _strict_upper(n):
    r = lax.broadcasted_iota(jnp.int32, (n, n), 0)
    c = lax.broadcasted_iota(jnp.int32, (n, n), 1)
    return (r < c).astype(BF16)


def _select_kernel(aff_ref, sel_ref, cnt_ref, pos_ref, *, cap):
    NT, E, tm = aff_ref.shape
    capf = float(cap)

    def count_ge(v):
        bits = lax.bitcast_convert_type(aff_ref[...], jnp.int32)
        ge = (bits >= v[None]).astype(F32)
        return jnp.sum(jnp.sum(ge, axis=0), axis=1, keepdims=True)

    def bisect(_, lh):
        lo, hi = lh
        mid = lo + lax.shift_right_logical(hi - lo, 1)
        ok = count_ge(mid) >= capf
        return jnp.where(ok, mid, lo), jnp.where(ok, hi, mid)

    one_bits = 0x3F800000
    lo0 = jnp.zeros((E, 1), jnp.int32)
    hi0 = jnp.full((E, 1), one_bits + 1, jnp.int32)
    tau, _ = lax.fori_loop(0, 31, bisect, (lo0, hi0))
    need = capf - count_ge(tau + 1)
    tri = _strict_upper(tm)

    def tile(j, carry):
        ceq, csel = carry
        bits = lax.bitcast_convert_type(aff_ref[j], jnp.int32)
        gt = bits > tau
        eq = (bits == tau).astype(F32)
        rank_eq = jnp.dot(eq.astype(BF16), tri, preferred_element_type=F32) + ceq
        sel = jnp.where(gt | ((eq > 0) & (rank_eq < need)), 1.0, 0.0)
        sel_ref[j] = sel
        c = jnp.sum(sel, axis=1, keepdims=True)
        cnt_ref[j] = jnp.broadcast_to(c, (E, LANES))
        pos_ref[j] = jnp.broadcast_to(csel, (E, LANES))
        return ceq + jnp.sum(eq, axis=1, keepdims=True), csel + c

    zero = jnp.zeros((E, 1), F32)
    lax.fori_loop(0, NT, tile, (zero, zero), unroll=4 if NT % 4 == 0 else 1)


def _select(aff, cap):
    NT, E, tm = aff.shape
    sel, cnt, pos = pl.pallas_call(
        functools.partial(_select_kernel, cap=cap),
        out_shape=[jax.ShapeDtypeStruct((NT, E, tm), F32),
                   jax.ShapeDtypeStruct((NT, E, LANES), F32),
                   jax.ShapeDtypeStruct((NT, E, LANES), F32)],
        compiler_params=pltpu.CompilerParams(vmem_limit_bytes=VMEM_LIMIT),
        name="expert_select",
    )(aff)
    to_smem = lambda a: a[:, :, 0].astype(jnp.int32).reshape(NT * E)
    return sel, to_smem(cnt), to_smem(pos)


def _stack_rows(E, tm, sub):
    worst = E * tm + E * (sub - 1 + PIECE - 1)
    return -(-worst // CHUNK) * CHUNK


def _cdiv_pow2(x, log2):
    return lax.shift_right_logical(x + ((1 << log2) - 1), log2)


def _segments(cnt_ref, pos_ref, tile, E, sub):
    segs = []
    base = jnp.int32(0)
    for e in range(E):
        n = cnt_ref[tile * E + e]
        p0 = pos_ref[tile * E + e]
        a = p0 & (sub - 1)
        pieces = jnp.where(n > 0, _cdiv_pow2(a + n, LOG2_PIECE), 0)
        segs.append((base, pieces, p0 - a, a, n))
        base = base + pieces * PIECE
    return segs, base


NO_RANK = -256.0
NO_OWNER = 1024.0


def _segment_rows(segs):
    lane = lax.broadcasted_iota(jnp.int32, (1, LANES), 1)
    first = jnp.full((1, LANES), float(NO_OWNER * NO_OWNER), F32)
    end = jnp.full((1, LANES), -1.0, F32)
    zero = jnp.zeros((1, LANES), F32)
    for e, (base, pieces, _, a, _) in enumerate(segs):
        m = lane == e
        first = jnp.where(m, base.astype(F32), first)
        end = jnp.where(m, (base + pieces * PIECE).astype(F32), end)
        zero = jnp.where(m, (base + a).astype(F32), zero)
    return first, end, zero


def _tile_ranks(sel, tm):
    E = sel.shape[0]
    rank = jnp.dot(sel.astype(BF16), _strict_upper(tm), preferred_element_type=F32)
    rank = jnp.where(sel > 0, rank, NO_RANK).astype(BF16)
    return jnp.concatenate([rank, jnp.zeros((LANES - E, tm), BF16)], axis=0)


def _chunk_one_hot(j0, rows, ranks, E):
    first, end, zero = rows
    jv = (j0 + lax.broadcasted_iota(jnp.int32, (CHUNK, LANES), 0)).astype(F32)
    own = (jv >= first) & (jv < end)
    lane = lax.broadcasted_iota(jnp.int32, (CHUNK, LANES), 1)
    tgt = jnp.sum(jnp.where(own & (lane < E), jv - zero + NO_OWNER, 0.0), axis=1, keepdims=True) - NO_OWNER
    ownb = jnp.where(own, 1.0, 0.0).astype(BF16)
    hit = jnp.dot(ownb, ranks, preferred_element_type=F32) == tgt
    return ownb, hit


def _total_pieces(segs):
    t = segs[0][1]
    for s in segs[1:]:
        t = t + s[1]
    return t


WAIT_GROUP = 8
LOG2_WAIT_GROUP = 3
assert 1 << LOG2_WAIT_GROUP == WAIT_GROUP


def _wait_pieces(n, copy_of_rows, unit=PIECE):
    def waits(count, rows):
        def w(_, c):
            copy_of_rows(rows).wait()
            return c
        lax.fori_loop(0, count, w, 0)
    waits(lax.shift_right_logical(n, LOG2_WAIT_GROUP), WAIT_GROUP * unit)
    waits(n & (WAIT_GROUP - 1), unit)


def _dispatch_kernel(cnt_ref, pos_ref, sel_ref, hn_ref, xe_ref, stack, pend, prev_copies, sem):
    i = pl.program_id(0)
    nt = pl.num_programs(0)
    slot = i % 2
    E, tm = sel_ref.shape

    @pl.when(i == 0)
    def _():
        pend[...] = jnp.zeros_like(pend)
        stack[...] = jnp.zeros_like(stack)

    segs, total = _segments(cnt_ref, pos_ref, i, E, SUB)
    rows = _segment_rows(segs)
    ranks = _tile_ranks(sel_ref[...], tm)

    def chunk(k, c):
        j0 = pl.multiple_of(k * CHUNK, CHUNK)
        _, hit = _chunk_one_hot(j0, rows, ranks, E)
        oh = jnp.where(hit, 1.0, 0.0).astype(BF16)
        stack[slot, pl.ds(j0, CHUNK), :] = jnp.dot(oh, hn_ref[...], preferred_element_type=F32)
        return c

    lax.fori_loop(0, _cdiv_pow2(total, LOG2_CHUNK), chunk, 0)

    spare = stack.shape[1] - PIECE
    for e in range(E):
        base, _, _, a, n = segs[e]
        live = n > 0
        first = pl.multiple_of(jnp.where(live, base, spare), SUB)
        last = pl.multiple_of(
            jnp.where(live, base + lax.shift_right_logical(a + n - 1, LOG2_SUB) * SUB, spare), SUB)
        stack[slot, pl.ds(first, SUB), :] = stack[slot, pl.ds(first, SUB), :] + jnp.where(live, pend[e], 0.0)
        incomplete = live & (((a + n) & (SUB - 1)) != 0)
        pend[e] = jnp.where(live, jnp.where(incomplete, stack[slot, pl.ds(last, SUB), :], 0.0), pend[e])

    def wide_copy(s, src_row, e, dst_row, rows=WIDE):
        return pltpu.make_async_copy(stack.at[s, pl.ds(src_row, rows), :],
                                     xe_ref.at[e, pl.ds(dst_row, rows), :], sem.at[s])

    def copies_of(some_segs):
        t = jnp.int32(0)
        for _, pieces, _, _, _ in some_segs:
            t = t + _cdiv_pow2(pieces, LOG2_WIDE_PIECES)
        return t

    waiter = lambda s: (lambda rows: wide_copy(s, 0, 0, 0, rows))

    n_copies = copies_of(segs)

    @pl.when(i > 0)
    def _():
        _wait_pieces(prev_copies[0], waiter(1 - slot), WIDE)

    prev_copies[0] = n_copies

    for e in range(E):
        base, pieces, start, _, _ = segs[e]

        def issue(c, carry, base=base, start=start, e=e):
            wide_copy(slot, pl.multiple_of(base + c * WIDE, PIECE), e,
                      pl.multiple_of(start + c * WIDE, SUB)).start()
            return carry

        lax.fori_loop(0, _cdiv_pow2(pieces, LOG2_WIDE_PIECES), issue, 0)

    @pl.when(i == nt - 1)
    def _():
        _wait_pieces(n_copies, waiter(slot), WIDE)
        cap = xe_ref.shape[1] - WIDE
        stack[slot, pl.ds(0, WIDE), :] = jnp.zeros((WIDE, stack.shape[2]), F32)
        for e in range(E):
            wide_copy(slot, 0, e, cap).start()
        _wait_pieces(E, waiter(slot), WIDE)


def _dispatch(sel, cnt, pos, hn, cap):
    NT, E, tm = sel.shape
    T, D = hn.shape
    return pl.pallas_call(
        _dispatch_kernel,
        grid_spec=pltpu.PrefetchScalarGridSpec(
            num_scalar_prefetch=2,
            grid=(NT,),
            in_specs=[pl.BlockSpec((None, E, tm), lambda i, c, p: (i, 0, 0)),
                      pl.BlockSpec((tm, D), lambda i, c, p: (i, 0))],
            out_specs=pl.BlockSpec(memory_space=pl.ANY),
            scratch_shapes=[pltpu.VMEM((2, _stack_rows(E, tm, SUB) + WIDE, D), F32),
                            pltpu.VMEM((E, SUB, D), F32),
                            pltpu.SMEM((1,), jnp.int32),
                            pltpu.SemaphoreType.DMA((2,))]),
        out_shape=jax.ShapeDtypeStruct((E, cap + WIDE, D), F32),
        compiler_params=_params(1),
        name="expert_dispatch",
    )(cnt, pos, sel, hn)


def _segment_cols(segs):
    E = len(segs)
    eidx = lax.broadcasted_iota(jnp.int32, (E, 1), 0)
    first = jnp.zeros((E, 1), F32)
    end = jnp.zeros((E, 1), F32)
    zero = jnp.zeros((E, 1), F32)
    for e, (base, pieces, _, a, _) in enumerate(segs):
        m = eidx == e
        first = jnp.where(m, base.astype(F32), first)
        end = jnp.where(m, (base + pieces * PIECE).astype(F32), end)
        zero = jnp.where(m, (base + a).astype(F32), zero)
    return first, end, zero


def _combine_kernel(cnt_ref, pos_ref, sel_ref, aff_ref, x_ref, y_hbm, *rest):
    g_ref = rest[0] if len(rest) == 4 else None
    o_ref, stack, sem = rest[-3:]
    i = pl.program_id(0)
    nt = pl.num_programs(0)
    slot = i % 2
    E, tm = sel_ref.shape

    def piece_copy(s, e, src_row, dst_row, rows=PIECE):
        return pltpu.make_async_copy(y_hbm.at[e, pl.ds(src_row, rows), :],
                                     stack.at[s, pl.ds(dst_row, rows), :], sem.at[s])

    def fetch(tile, s):
        segs, _ = _segments(cnt_ref, pos_ref, tile, E, SUB_BF16)
        for e in range(E):
            base, pieces, start, _, _ = segs[e]

            def issue(c, carry, base=base, start=start, e=e):
                piece_copy(s, e, pl.multiple_of(start + c * PIECE, SUB_BF16),
                           pl.multiple_of(base + c * PIECE, PIECE)).start()
                return carry

            lax.fori_loop(0, pieces, issue, 0)

    @pl.when(i == 0)
    def _():
        stack[...] = jnp.zeros_like(stack)
        fetch(0, 0)

    @pl.when(i + 1 < nt)
    def _():
        fetch(i + 1, 1 - slot)

    segs, total = _segments(cnt_ref, pos_ref, i, E, SUB_BF16)
    first, end, zero = _segment_cols(segs)
    sel = sel_ref[...]
    rank = jnp.dot(sel.astype(BF16), _strict_upper(tm), preferred_element_type=F32)
    rank = jnp.where(sel > 0, rank, NO_RANK)
    pad = jnp.zeros((LANES - 2 * E, tm), F32)
    by_token = jnp.transpose(jnp.concatenate([aff_ref[...], rank, pad], axis=0)).astype(BF16)
    o_ref[...] = x_ref[...]
    _wait_pieces(_total_pieces(segs), lambda rows: piece_copy(slot, 0, 0, 0, rows))

    def chunk(k, c):
        j0 = pl.multiple_of(k * CHUNK, CHUNK)
        jl = (j0 + lax.broadcasted_iota(jnp.int32, (E, CHUNK), 1)).astype(F32)
        own = (jl >= first) & (jl < end)
        tgt = jnp.sum(jnp.where(own, jl - zero + NO_OWNER, 0.0), axis=0, keepdims=True) - NO_OWNER
        o16 = jnp.where(own, 1.0, 0.0).astype(BF16)
        zeros = lambda n: jnp.zeros((n, CHUNK), BF16)
        own_aff = jnp.concatenate([o16, zeros(LANES - E)], axis=0)
        own_rank = jnp.concatenate([zeros(E), o16, zeros(LANES - 2 * E)], axis=0)
        hit = jnp.dot(by_token, own_rank, preferred_element_type=F32) == tgt
        g = jnp.where(hit, jnp.dot(by_token, own_aff, preferred_element_type=F32), 0.0).astype(BF16)
        o_ref[...] += jnp.dot(g, stack[slot, pl.ds(j0, CHUNK), :], preferred_element_type=F32)
        return c

    lax.fori_loop(0, _cdiv_pow2(total, LOG2_CHUNK), chunk, 0)
    if g_ref is not None:
        o_ref[...] = _rms(o_ref[...], g_ref[...])


def _combine(sel, aff, cnt, pos, x, y, final_g=None):
    NT, E, tm = sel.shape
    T, D = x.shape
    tile_spec = pl.BlockSpec((None, E, tm), lambda i, c, p: (i, 0, 0))
    x_spec = pl.BlockSpec((tm, D), lambda i, c, p: (i, 0))
    extra_specs = [] if final_g is None else [pl.BlockSpec((1, D), lambda i, c, p: (0, 0))]
    extra = [] if final_g is None else [final_g]
    return pl.pallas_call(
        _combine_kernel,
        grid_spec=pltpu.PrefetchScalarGridSpec(
            num_scalar_prefetch=2,
            grid=(NT,),
            in_specs=[tile_spec, tile_spec, x_spec, pl.BlockSpec(memory_space=pl.ANY)] + extra_specs,
            out_specs=x_spec,
            scratch_shapes=[pltpu.VMEM((2, _stack_rows(E, tm, SUB_BF16), D), BF16),
                            pltpu.SemaphoreType.DMA((2,))]),
        out_shape=jax.ShapeDtypeStruct((T, D), F32),
        compiler_params=_params(1),
        name="expert_combine" if final_g is None else "expert_combine_norm",
    )(cnt, pos, sel, aff, x, y, *extra)


def _ffn_kernel(x_ref, wg_hbm, wu_hbm, wd_hbm, y_ref, wg_buf, wu_buf, wd_buf, wg_stg, wu_stg, wd_stg, sem,
                *, layer):
    e = pl.program_id(0)
    i = pl.program_id(1)
    n_e = pl.num_programs(0)
    n = pl.num_programs(1) - 1
    slot = e % 2
    stages = ((wg_hbm, wg_stg, wg_buf), (wu_hbm, wu_stg, wu_buf), (wd_hbm, wd_stg, wd_buf))

    def chunk_copy(k, expert, c):
        hbm, stg, _ = stages[k]
        rows = stg.shape[0]
        return pltpu.make_async_copy(hbm.at[layer, expert, pl.ds(pl.multiple_of(c * rows, rows), rows), :],
                                     stg, sem.at[k])

    def land(expert, c, dst_slot):
        for k, (_, stg, buf) in enumerate(stages):
            rows = stg.shape[0]
            chunk_copy(k, expert, c).wait()
            buf[dst_slot, pl.ds(pl.multiple_of(c * rows, rows), rows), :] = stg[...].astype(BF16)

    @pl.when((e == 0) & (i == 0))
    def _():
        def first(c, carry):
            for k in range(3):
                chunk_copy(k, 0, c).start()
            land(0, c, 0)
            return carry
        lax.fori_loop(0, n, first, 0)

    nxt = jnp.minimum(e + 1, n_e - 1)

    @pl.when((e + 1 < n_e) & (i > 0))
    def _():
        land(nxt, i - 1, 1 - slot)

    @pl.when((e + 1 < n_e) & (i < n))
    def _():
        for k in range(3):
            chunk_copy(k, nxt, i).start()

    @pl.when(i < n)
    def _():
        x = x_ref[...].astype(BF16)
        a = jnp.dot(x, wg_buf[slot], preferred_element_type=F32)
        b = jnp.dot(x, wu_buf[slot], preferred_element_type=F32)
        h = (a * (1.0 / (1.0 + jnp.exp(-a)))) * b
        y_ref[...] = jnp.dot(h.astype(BF16), wd_buf[slot], preferred_element_type=F32).astype(y_ref.dtype)

    @pl.when(i == n)
    def _():
        y_ref[...] = jnp.zeros_like(y_ref)


def _expert_ffn(xe, cap, wg, wu, wd, layer):
    E, _, D = xe.shape
    F = wg.shape[3]
    tm = min(ROW_TILE, cap)
    n = cap // tm
    any_spec = pl.BlockSpec(memory_space=pl.ANY)
    return pl.pallas_call(
        functools.partial(_ffn_kernel, layer=layer),
        grid=(E, n + 1),
        in_specs=[pl.BlockSpec((None, tm, D), lambda e, i: (e, jnp.minimum(i, n - 1), 0)),
                  any_spec, any_spec, any_spec],
        out_specs=pl.BlockSpec((None, tm, D), lambda e, i: (e, i, 0)),
        out_shape=jax.ShapeDtypeStruct((E, cap + tm, D), BF16),
        scratch_shapes=[pltpu.VMEM((2, D, F), BF16), pltpu.VMEM((2, D, F), BF16), pltpu.VMEM((2, F, D), BF16),
                        pltpu.VMEM((D // n, F), F32), pltpu.VMEM((D // n, F), F32), pltpu.VMEM((F // n, D), F32),
                        pltpu.SemaphoreType.DMA((3,))],
        compiler_params=_params(2),
        name=f"expert_ffn_cap{cap}",
    )(xe, wg, wu, wd)


def _moe(x, hn, aff, wg, wu, wd, layer, final_g):
    T, D = x.shape
    cap = max(1, (EC_CAPACITY * T) // N_EXPERTS)
    sel, cnt, pos = _select(aff, cap)
    xe = _dispatch(sel, cnt, pos, hn, cap)
    y = _expert_ffn(xe, cap, wg, wu, wd, layer)
    return _combine(sel, aff, cnt, pos, x, y, final_g)


def _trunk(x3, p):
    B, S, D = x3.shape
    x = x3.reshape(B * S, D)
    tables = _rope_tables(S)
    depth = p["ffn_norm"].shape[0]
    for i in range(depth):
        j = i // 2
        g_ffn, w_rt = p["ffn_norm"][i][None], p["w_router_t"][i]
        if i % 2 == 0:
            qkvs = _qkv_proj(x, p["attn_norm"][j][None], p["w_qkv"][j], tables, B, S)
            res = [_dilated_attention(qkvs[g], g) for g in range(N_GROUPS)]
            x, hn, aff = _attn_out_proj([r[0] for r in res], [r[1] for r in res], p["w_attn_out"][j], x,
                                        g_ffn, w_rt, S)
        else:
            x, hn, aff = _short_conv(x, p["conv_norm"][j][None], p["w_conv_in"][j], p["conv_w"][j],
                                     p["w_conv_out"][j], g_ffn, w_rt, S)
        final_g = p["final_norm"][None] if i == depth - 1 else None
        x = _moe(x, hn, aff, p["w_gate"], p["w_up"], p["w_down"], i, final_g)
    return x.reshape(B, S, D)


def kernel(x_prompt, x_sample, attn_norm, w_qkv, w_attn_out, conv_norm, w_conv_in, conv_w, w_conv_out,
           ffn_norm, w_router, w_gate, w_up, w_down, final_norm):
    per_layer = lambda w: [w[i].astype(BF16) for i in range(w.shape[0])]
    p = dict(
        attn_norm=attn_norm, conv_norm=conv_norm, ffn_norm=ffn_norm, final_norm=final_norm, conv_w=conv_w,
        w_qkv=per_layer(w_qkv), w_attn_out=per_layer(w_attn_out),
        w_conv_in=per_layer(w_conv_in), w_conv_out=per_layer(w_conv_out),
        w_router_t=per_layer(jnp.swapaxes(w_router, 1, 2)),
        w_gate=w_gate, w_up=w_up, w_down=w_down,
    )
    return (_trunk(x_prompt, p), _trunk(x_sample, p))
```
